```python
import math
import jax
import jax.numpy as jnp
from jax import lax
import numpy as np

D_MODEL = 1024
BATCH = 2
SEQ = 16384
DEPTH = 4

CTX_LEN = 256
GRID_W = 64

RET_HEADS = 4
RET_DK = 128
RET_DV = 256
RET_CHUNK = 128
RET_QK = RET_HEADS * RET_DK
RET_V = RET_HEADS * RET_DV

DN_HEADS = 4
DN_DK = 128
DN_DV = 256
DN_CHUNK = 64
DN_CONV = 5
DN_QK = DN_HEADS * DN_DK
DN_V = DN_HEADS * DN_DV
DN_CONV_CH = 2 * DN_QK + DN_V

FFN_HIDDEN = 2816

ROPE_BASE = 10000.0
EPS = 1e-6
N_MOD = 9

IN_SIZES = (RET_QK, RET_QK, RET_V, RET_V, DN_QK, DN_QK, DN_V, DN_V, 4 * DN_HEADS, D_MODEL, D_MODEL)
IN_WIDTH = sum(IN_SIZES)

kernel_name = "hybrid_retention_deltanet_prefix_dit"


def _split_points(sizes):
    pts, acc = [], 0
    for s in sizes[:-1]:
        acc += s
        pts.append(acc)
    return pts


def rmsnorm(x, w):
    xf = x.astype(jnp.float32)
    y = xf * lax.rsqrt(jnp.mean(xf * xf, axis=-1, keepdims=True) + EPS)
    return (y * w.astype(jnp.float32)).astype(x.dtype)


def modulate(h, shift, scale):
    return h * (1.0 + scale) + shift


def swiglu(h, w_gu, w_down):
    gate, up = jnp.split(h @ w_gu, 2, axis=-1)
    return (jax.nn.silu(gate) * up) @ w_down


def _heads(t, n_heads):
    b, l, _ = t.shape
    return t.reshape(b, l, n_heads, -1).transpose(0, 2, 1, 3).astype(jnp.float32)


def _merge_heads(t):
    b, h, l, d = t.shape
    return t.transpose(0, 2, 1, 3).reshape(b, l, h * d)


def _flip(t):
    return jnp.flip(t, axis=2)


def l2norm(t):
    return t * lax.rsqrt(jnp.sum(t * t, axis=-1, keepdims=True) + EPS)


def axial_rope(rows):
    half = RET_DK // 2
    n_freq = half // 2
    inv = ROPE_BASE ** (-jnp.arange(n_freq, dtype=jnp.float32) / n_freq)
    ang_r = jnp.arange(rows, dtype=jnp.float32)[:, None] * inv
    ang_c = jnp.arange(GRID_W, dtype=jnp.float32)[:, None] * inv
    ang = jnp.concatenate([
        jnp.broadcast_to(ang_r[:, None, :], (rows, GRID_W, n_freq)),
        jnp.broadcast_to(ang_c[None, :, :], (rows, GRID_W, n_freq)),
    ], axis=-1).reshape(rows * GRID_W, half)
    return jnp.cos(ang), jnp.sin(ang)


def apply_rope(t, cos, sin):
    half = t.shape[-1] // 2
    t1, t2 = t[..., :half], t[..., half:]
    return jnp.concatenate([t1 * cos - t2 * sin, t2 * cos + t1 * sin], axis=-1)


def short_conv(u, w):
    pad = DN_CONV // 2
    y = lax.conv_general_dilated(
        u, w[:, None, :], window_strides=(1,), padding=[(pad, pad)],
        dimension_numbers=("NWC", "WIO", "NWC"), feature_group_count=u.shape[-1])
    return jax.nn.silu(y)


def retention_chunked(q, k, v, log_gamma, s0, include_diag):
    b, h, l, dk = q.shape
    dv = v.shape[-1]
    c = RET_CHUNK
    n = l // c
    qc = q.reshape(b, h, n, c, dk)
    kc = k.reshape(b, h, n, c, dk)
    vc = v.reshape(b, h, n, c, dv)
    idx = jnp.arange(c, dtype=jnp.float32)
    diff = idx[:, None] - idx[None, :]
    mask = diff >= 0 if include_diag else diff > 0
    decay = jnp.where(mask, jnp.exp(log_gamma[:, None, None] * jnp.where(mask, diff, 0.0)), 0.0)
    scores = jnp.einsum("bhncd,bhnmd->bhncm", qc, kc) * decay[None, :, None]
    o_intra = jnp.einsum("bhncm,bhnme->bhnce", scores, vc)
    k_dec = kc * jnp.exp(log_gamma[:, None] * (c - 1.0 - idx)[None, :])[None, :, None, :, None]
    chunk_kv = jnp.einsum("bhncd,bhnce->bhnde", k_dec, vc)
    gamma_c = jnp.exp(log_gamma * c)[None, :, None, None]

    def step(s, kv):
        return s * gamma_c + kv, s

    s_fin, s_prev = lax.scan(step, s0, jnp.moveaxis(chunk_kv, 2, 0))
    s_prev = jnp.moveaxis(s_prev, 0, 2)
    q_dec = qc * jnp.exp(log_gamma[:, None] * (idx + 1.0)[None, :])[None, :, None, :, None]
    o_inter = jnp.einsum("bhncd,bhnde->bhnce", q_dec, s_prev)
    return (o_intra + o_inter).reshape(b, h, l, dv), s_fin


def gated_delta_chunked(q, k, v, g, beta, s0):
    b, h, l, dk = q.shape
    dv = v.shape[-1]
    c = DN_CHUNK
    n = l // c
    q = q.reshape(b, h, n, c, dk)
    k = k.reshape(b, h, n, c, dk)
    v = v.reshape(b, h, n, c, dv)
    g = g.reshape(b, h, n, c)
    beta = beta.reshape(b, h, n, c)
    gc = jnp.cumsum(g, axis=-1)
    incl = jnp.tril(jnp.ones((c, c), dtype=bool))
    strict = jnp.tril(jnp.ones((c, c), dtype=bool), -1)
    gdiff = gc[..., :, None] - gc[..., None, :]
    decay = jnp.where(incl, jnp.exp(jnp.where(incl, gdiff, 0.0)), 0.0)
    kb = k * beta[..., None]
    a = jnp.where(strict, jnp.einsum("bhnid,bhnjd->bhnij", kb, k) * decay, 0.0)
    eye = jnp.eye(c, dtype=jnp.float32)
    t = lax.linalg.triangular_solve(eye + a, jnp.broadcast_to(eye, a.shape),
                                    left_side=True, lower=True, unit_diagonal=True)
    u = jnp.einsum("bhnij,bhnje->bhnie", t, v * beta[..., None])
    w = jnp.einsum("bhnij,bhnjd->bhnid", t, kb * jnp.exp(gc)[..., None])
    attn = jnp.where(incl, jnp.einsum("bhnid,bhnjd->bhnij", q, k) * decay, 0.0)
    qg = q * jnp.exp(gc)[..., None]
    glast = gc[..., -1]
    kd = k * jnp.exp(glast[..., None] - gc)[..., None]

    def step(s, xs):
        u_n, w_n, qg_n, attn_n, kd_n, gl_n = xs
        v_new = u_n - jnp.einsum("bhcd,bhde->bhce", w_n, s)
        o = jnp.einsum("bhcd,bhde->bhce", qg_n, s) + jnp.einsum("bhij,bhje->bhie", attn_n, v_new)
        s = s * jnp.exp(gl_n)[..., None, None] + jnp.einsum("bhcd,bhce->bhde", kd_n, v_new)
        return s, o

    xs = tuple(jnp.moveaxis(z, 2, 0) for z in (u, w, qg, attn, kd, glast))
    s_fin, o = lax.scan(step, s0, xs)
    return jnp.moveaxis(o, 0, 2).reshape(b, h, l, dv), s_fin


def mixer_features(p, cos_sin, conv_w, a_log, dt_bias):
    b, l, _ = p.shape
    (rq, rk, rv, rg, dn_q, dn_k, dn_v, dz, dba, gate_a, gate_b) = jnp.split(p, _split_points(IN_SIZES), axis=-1)
    rq = _heads(rq, RET_HEADS)
    rk = _heads(rk, RET_HEADS) * (RET_DK ** -0.5)
    if cos_sin is not None:
        cos, sin = cos_sin
        rq = apply_rope(rq, cos, sin)
        rk = apply_rope(rk, cos, sin)
    rv = _heads(rv, RET_HEADS)
    qkv = short_conv(jnp.concatenate([dn_q, dn_k, dn_v], axis=-1).astype(jnp.float32), conv_w.astype(jnp.float32))
    dn_q, dn_k, dn_v = jnp.split(qkv, [DN_QK, 2 * DN_QK], axis=-1)
    dn_q = l2norm(_heads(dn_q, DN_HEADS)) * (DN_DK ** -0.5)
    dn_k = l2norm(_heads(dn_k, DN_HEADS))
    dn_v = _heads(dn_v, DN_HEADS)
    ba = dba.astype(jnp.float32).reshape(b, l, 4, DN_HEADS).transpose(2, 0, 3, 1)
    beta = jax.nn.sigmoid(ba[:2])
    g = -jnp.exp(a_log.astype(jnp.float32))[:, None, :, None] * jax.nn.softplus(
        ba[2:] + dt_bias.astype(jnp.float32)[:, None, :, None])
    return (rq, rk, rv, dn_q, dn_k, dn_v, beta, g), (rg, dz, gate_a, gate_b)


def bidirectional_mix(fc, fx, log_gamma):
    crq, crk, crv, cdq, cdk, cdv, cbeta, cg = fc
    xrq, xrk, xrv, xdq, xdk, xdv, xbeta, xg = fx
    b = crq.shape[0]
    s_ret0 = jnp.zeros((b, RET_HEADS, RET_DK, RET_DV), jnp.float32)
    s_dn0 = jnp.zeros((b, DN_HEADS, DN_DK, DN_DV), jnp.float32)
    f = _flip
    rc_f, sr_f = retention_chunked(crq, crk, crv, log_gamma, s_ret0, True)
    rc_b, sr_b = retention_chunked(f(crq), f(crk), f(crv), log_gamma, s_ret0, False)
    rx_f, _ = retention_chunked(xrq, xrk, xrv, log_gamma, sr_f, True)
    rx_b, _ = retention_chunked(f(xrq), f(xrk), f(xrv), log_gamma, sr_b, False)
    dc_f, sd_f = gated_delta_chunked(cdq, cdk, cdv, cg[0], cbeta[0], s_dn0)
    dc_b, sd_b = gated_delta_chunked(f(cdq), f(cdk), f(cdv), f(cg[1]), f(cbeta[1]), s_dn0)
    dx_f, _ = gated_delta_chunked(xdq, xdk, xdv, xg[0], xbeta[0], sd_f)
    dx_b, _ = gated_delta_chunked(f(xdq), f(xdk), f(xdv), f(xg[1]), f(xbeta[1]), sd_b)
    ctx_out = (rc_f + f(rc_b), dc_f + f(dc_b))
    lat_out = (rx_f + f(rx_b), dx_f + f(dx_b))
    return ctx_out, lat_out


def mixer_output(ret_o, dn_o, gates, dn_norm_w, w_ret_out, w_dn_out, w_o, dtype):
    rg, dz, gate_a, gate_b = gates
    ret = ret_o * lax.rsqrt(jnp.mean(ret_o * ret_o, axis=-1, keepdims=True) + EPS)
    ret = _merge_heads(ret).astype(dtype) * jax.nn.silu(rg)
    dn = dn_o * lax.rsqrt(jnp.mean(dn_o * dn_o, axis=-1, keepdims=True) + EPS) * dn_norm_w.astype(jnp.float32)
    dn = _merge_heads(dn).astype(dtype) * jax.nn.silu(dz)
    y = jax.nn.sigmoid(gate_a) * (ret @ w_ret_out) + jax.nn.sigmoid(gate_b) * (dn @ w_dn_out)
    return y @ w_o


def setup_inputs(seed: int = 0) -> dict:
    key = jax.random.key(seed)
    ks = jax.random.split(key, 24)
    f32 = jnp.float32

    def nrm(k, shape, fan_in, gain=1.0):
        return jax.random.normal(k, shape, f32) * (gain * fan_in ** -0.5)

    a_init = jax.random.uniform(ks[10], (DEPTH, 2, DN_HEADS), f32, 1.0, 16.0)
    dt = jnp.exp(jax.random.uniform(ks[11], (DEPTH, 2, DN_HEADS), f32, math.log(1e-3), math.log(1e-1)))
    dt_bias = dt + jnp.log(-jnp.expm1(-dt))
    return {
        "x": jax.random.normal(ks[0], (BATCH, SEQ, D_MODEL), f32),
        "c": jax.random.normal(ks[1], (BATCH, D_MODEL), f32),
        "ctx": jax.random.normal(ks[2], (BATCH, CTX_LEN, D_MODEL), f32),
        "c_ctx": jax.random.normal(ks[3], (D_MODEL,), f32),
        "ada_w": nrm(ks[4], (DEPTH, D_MODEL, N_MOD * D_MODEL), D_MODEL, 0.5),
        "ada_b": 0.02 * jax.random.normal(ks[5], (DEPTH, N_MOD * D_MODEL), f32),
        "norm_w": 1.0 + 0.05 * jax.random.normal(ks[6], (DEPTH, 3, D_MODEL), f32),
        "ffn1_wgu": nrm(ks[7], (DEPTH, D_MODEL, 2 * FFN_HIDDEN), D_MODEL),
        "ffn1_wd": nrm(ks[8], (DEPTH, FFN_HIDDEN, D_MODEL), FFN_HIDDEN),
        "w_in": nrm(ks[9], (DEPTH, D_MODEL, IN_WIDTH), D_MODEL),
        "dn_conv_w": nrm(ks[12], (DEPTH, DN_CONV, DN_CONV_CH), DN_CONV),
        "dn_a_log": jnp.log(a_init),
        "dn_dt_bias": dt_bias,
        "dn_norm_w": 1.0 + 0.05 * jax.random.normal(ks[13], (DEPTH, DN_DV), f32),
        "w_ret_out": nrm(ks[14], (DEPTH, RET_V, D_MODEL), RET_V),
        "w_dn_out": nrm(ks[15], (DEPTH, DN_V, D_MODEL), DN_V),
        "w_o": nrm(ks[16], (DEPTH, D_MODEL, D_MODEL), D_MODEL),
        "ffn2_wgu": nrm(ks[17], (DEPTH, D_MODEL, 2 * FFN_HIDDEN), D_MODEL),
        "ffn2_wd": nrm(ks[18], (DEPTH, FFN_HIDDEN, D_MODEL), FFN_HIDDEN),
        "final_norm_w": 1.0 + 0.05 * jax.random.normal(ks[19], (D_MODEL,), f32),
    }


def reference(x, c, ctx, c_ctx, ada_w, ada_b, norm_w, ffn1_wgu, ffn1_wd, w_in, dn_conv_w,
              dn_a_log, dn_dt_bias, dn_norm_w, w_ret_out, w_dn_out, w_o, ffn2_wgu, ffn2_wd,
              final_norm_w):
    b, l, _ = x.shape
    rows = l // GRID_W
    rope = axial_rope(rows)
    log_gamma = jnp.log1p(-jnp.power(2.0, -5.0 - jnp.arange(RET_HEADS, dtype=jnp.float32)))
    silu_c = jax.nn.silu(c)
    silu_cc = jax.nn.silu(c_ctx)[None, :]
    cx = ctx
    for layer in range(DEPTH):
        last = layer == DEPTH - 1
        mx = jnp.split((silu_c @ ada_w[layer] + ada_b[layer])[:, None, :], N_MOD, axis=-1)
        mc = jnp.split((silu_cc @ ada_w[layer] + ada_b[layer])[:, None, :], N_MOD, axis=-1)

        x = x + 0.5 * mx[2] * swiglu(modulate(rmsnorm(x, norm_w[layer, 0]), mx[0], mx[1]),
                                     ffn1_wgu[layer], ffn1_wd[layer])
        cx = cx + 0.5 * mc[2] * swiglu(modulate(rmsnorm(cx, norm_w[layer, 0]), mc[0], mc[1]),
                                       ffn1_wgu[layer], ffn1_wd[layer])

        hx = modulate(rmsnorm(x, norm_w[layer, 1]), mx[3], mx[4])
        hc = modulate(rmsnorm(cx, norm_w[layer, 1]), mc[3], mc[4])
        fx, gx = mixer_features(hx @ w_in[layer], rope, dn_conv_w[layer], dn_a_log[layer], dn_dt_bias[layer])
        fc, gc = mixer_features(hc @ w_in[layer], None, dn_conv_w[layer], dn_a_log[layer], dn_dt_bias[layer])
        (ret_c, dn_c), (ret_x, dn_x) = bidirectional_mix(fc, fx, log_gamma)
        x = x + mx[5] * mixer_output(ret_x, dn_x, gx, dn_norm_w[layer], w_ret_out[layer],
                                     w_dn_out[layer], w_o[layer], x.dtype)

        x = x + 0.5 * mx[8] * swiglu(modulate(rmsnorm(x, norm_w[layer, 2]), mx[6], mx[7]),
                                     ffn2_wgu[layer], ffn2_wd[layer])
        if not last:
            cx = cx + mc[5] * mixer_output(ret_c, dn_c, gc, dn_norm_w[layer], w_ret_out[layer],
                                           w_dn_out[layer], w_o[layer], cx.dtype)
            cx = cx + 0.5 * mc[8] * swiglu(modulate(rmsnorm(cx, norm_w[layer, 2]), mc[6], mc[7]),
                                           ffn2_wgu[layer], ffn2_wd[layer])
    return rmsnorm(x, final_norm_w)
```

```python
import functools
import math

import jax
import jax.numpy as jnp
from jax import lax
from jax.experimental import pallas as pl
from jax.experimental.pallas import tpu as pltpu

F32 = jnp.float32
BF16 = jnp.bfloat16

D_MODEL = 1024
FFN_HIDDEN = 2816
N_MOD = 9
GRID_W = 64
ROPE_BASE = 10000.0
EPS = 1e-6

HEADS = 4
DK = 128
DV = 256
QK = HEADS * DK
VW = HEADS * DV
DN_CONV = 5
CONV_CH = 2 * QK + VW
N_BA = 4 * HEADS

LANES = 128
SUBLANES = 8
MXU_N = 256
VMEM_LIMIT = 56 * 1024 * 1024

DN_CHUNK = 128
FFN_CHUNK = MXU_N


def _dot(a, b):
    return jnp.dot(a, b, preferred_element_type=F32)


def _dot_nt(a, b):
    return lax.dot_general(a, b, (((1,), (1,)), ((), ())), preferred_element_type=F32)


def _silu(x):
    return x * jax.nn.sigmoid(x)


def _row_tile(l, lc_total):
    for t in (512, 256, 128):
        if l % t == 0 and lc_total % t == 0:
            return t
    raise ValueError("sequence lengths must be multiples of 128")


def _params(sem):
    return pltpu.CompilerParams(dimension_semantics=sem, vmem_limit_bytes=VMEM_LIMIT)


def _const_spec(shape):
    nd = len(shape)
    return pl.BlockSpec(shape, lambda *_: (0,) * nd, pipeline_mode=pl.Buffered(1))


def _ada_kernel(c_ref, w_ref, b_ref, o_ref):
    c = c_ref[...]
    s = _silu(c)
    o_ref[0] = jnp.dot(s, w_ref[0], preferred_element_type=F32,
                       precision=lax.Precision.HIGHEST) + b_ref[0]


def _ada_all(cmat, ada_w, ada_b):
    depth = ada_w.shape[0]
    n = ada_w.shape[2]
    tn = 1024
    return pl.pallas_call(
        _ada_kernel,
        grid=(depth, n // tn),
        in_specs=[
            pl.BlockSpec((SUBLANES, D_MODEL), lambda l, j: (0, 0)),
            pl.BlockSpec((1, D_MODEL, tn), lambda l, j: (l, 0, j)),
            pl.BlockSpec((1, 1, tn), lambda l, j: (l, 0, j)),
        ],
        out_specs=pl.BlockSpec((1, SUBLANES, tn), lambda l, j: (l, 0, j)),
        out_shape=jax.ShapeDtypeStruct((depth, SUBLANES, n), F32),
        compiler_params=_params(("parallel", "parallel")),
        name="ada",
    )(cmat, ada_w, ada_b.reshape(depth, 1, n))


def _norm_mod(x, nw, shift, scale):
    ms = jnp.mean(x * x, axis=-1, keepdims=True)
    h = x * lax.rsqrt(ms + EPS) * nw
    return h * (1.0 + scale) + shift


def _mod_spec(tiles_per_batch, batch):
    return pl.BlockSpec((1, N_MOD, D_MODEL), lambda i: (jnp.minimum(i // tiles_per_batch, batch), 0, 0))


def _ffn_kernel(x_ref, m_ref, nw_ref, wgu_ref, wd_ref, o_ref, *, k0):
    x = x_ref[...]
    shift = m_ref[0, k0:k0 + 1, :]
    scale = m_ref[0, k0 + 1:k0 + 2, :]
    gate = m_ref[0, k0 + 2:k0 + 3, :]
    h = _norm_mod(x, nw_ref[...], shift, scale).astype(BF16)
    acc = jnp.zeros(x.shape, F32)
    for c in range(FFN_HIDDEN // FFN_CHUNK):
        lo = c * FFN_CHUNK
        g = _dot(h, wgu_ref[:, lo:lo + FFN_CHUNK])
        u = _dot(h, wgu_ref[:, FFN_HIDDEN + lo:FFN_HIDDEN + lo + FFN_CHUNK])
        a = (_silu(g) * u).astype(BF16)
        acc = acc + _dot(a, wd_ref[lo:lo + FFN_CHUNK, :])
    o_ref[...] = x + (0.5 * gate) * acc


def _ffn(xs, mods, nw, wgu, wd, k0, tm, tiles_per_batch, batch):
    nt = xs.shape[0]
    return pl.pallas_call(
        functools.partial(_ffn_kernel, k0=k0),
        grid=(nt // tm,),
        in_specs=[
            pl.BlockSpec((tm, D_MODEL), lambda i: (i, 0)),
            _mod_spec(tiles_per_batch, batch),
            _const_spec((1, D_MODEL)),
            _const_spec((D_MODEL, 2 * FFN_HIDDEN)),
            _const_spec((FFN_HIDDEN, D_MODEL)),
        ],
        out_specs=pl.BlockSpec((tm, D_MODEL), lambda i: (i, 0)),
        out_shape=jax.ShapeDtypeStruct((nt, D_MODEL), F32),
        compiler_params=_params(("parallel",)),
        name="ffn",
    )(xs, mods, nw.reshape(1, D_MODEL), wgu, wd)


_OFF_RQ, _OFF_RK, _OFF_RV, _OFF_RG = 0, QK, 2 * QK, 2 * QK + VW
_OFF_DQKV = 2 * QK + 2 * VW
_OFF_DZ = _OFF_DQKV + CONV_CH
_OFF_GA = _OFF_DZ + VW
_OFF_GB = _OFF_GA + D_MODEL
_W_MAIN = _OFF_GB + D_MODEL


def _rope(p, cos, sin):
    outs = []
    for hh in range(HEADS):
        slab = p[:, hh * DK:(hh + 1) * DK]
        outs.append(slab * cos + pltpu.roll(slab, DK // 2, 1) * sin)
    return jnp.concatenate(outs, axis=1)


def _inproj_kernel(x_ref, m_ref, nw_ref, w_ref, wba_ref, wbat_ref, cos_ref, sin_ref,
                   rq_ref, rk_ref, rv_ref, srg_ref, dpre_ref, sdz_ref, ba_ref, bat_ref, sga_ref, sgb_ref):
    x = x_ref[...]
    h = _norm_mod(x, nw_ref[...], m_ref[0, 3:4, :], m_ref[0, 4:5, :]).astype(BF16)
    cos = cos_ref[...]
    sin = sin_ref[...]
    rq_ref[...] = _rope(_dot(h, w_ref[:, _OFF_RQ:_OFF_RQ + QK]), cos, sin).astype(BF16)
    rk_ref[...] = _rope(_dot(h, w_ref[:, _OFF_RK:_OFF_RK + QK]) * (DK ** -0.5), cos, sin).astype(BF16)
    rv_ref[...] = _dot(h, w_ref[:, _OFF_RV:_OFF_RV + VW]).astype(BF16)
    srg_ref[...] = _silu(_dot(h, w_ref[:, _OFF_RG:_OFF_RG + VW])).astype(BF16)
    half = CONV_CH // 2
    dpre_ref[:, :half] = _dot(h, w_ref[:, _OFF_DQKV:_OFF_DQKV + half])
    dpre_ref[:, half:] = _dot(h, w_ref[:, _OFF_DQKV + half:_OFF_DQKV + CONV_CH])
    sdz_ref[...] = _silu(_dot(h, w_ref[:, _OFF_DZ:_OFF_DZ + VW])).astype(BF16)
    sga_ref[...] = jax.nn.sigmoid(_dot(h, w_ref[:, _OFF_GA:_OFF_GA + D_MODEL])).astype(BF16)
    sgb_ref[...] = jax.nn.sigmoid(_dot(h, w_ref[:, _OFF_GB:_OFF_GB + D_MODEL])).astype(BF16)
    ba_ref[...] = _dot(h, wba_ref[...])
    bat_ref[...] = _dot_nt(wbat_ref[...], h)


def _inproj(xs, mods, nw, w_main, w_ba, w_bat, cos_t, sin_t, tm, tiles_per_batch, batch):
    nt = xs.shape[0]
    n_lat_tiles = tiles_per_batch * batch

    def rope_map(i):
        return (jnp.where(i < n_lat_tiles, i % tiles_per_batch, tiles_per_batch), 0)

    row = lambda w: pl.BlockSpec((tm, w), lambda i: (i, 0))
    outs = [
        (QK, BF16), (QK, BF16), (VW, BF16), (VW, BF16), (CONV_CH, F32), (VW, BF16),
        (LANES, F32), None, (D_MODEL, BF16), (D_MODEL, BF16),
    ]
    out_specs, out_shape = [], []
    for o in outs:
        if o is None:
            out_specs.append(pl.BlockSpec((N_BA, tm), lambda i: (0, i)))
            out_shape.append(jax.ShapeDtypeStruct((N_BA, nt), F32))
        else:
            out_specs.append(row(o[0]))
            out_shape.append(jax.ShapeDtypeStruct((nt, o[0]), o[1]))
    return pl.pallas_call(
        _inproj_kernel,
        grid=(nt // tm,),
        in_specs=[
            row(D_MODEL),
            _mod_spec(tiles_per_batch, batch),
            _const_spec((1, D_MODEL)),
            _const_spec((D_MODEL, _W_MAIN)),
            _const_spec((D_MODEL, LANES)),
            _const_spec((N_BA, D_MODEL)),
            pl.BlockSpec((tm, DK), rope_map),
            pl.BlockSpec((tm, DK), rope_map),
        ],
        out_specs=out_specs,
        out_shape=out_shape,
        compiler_params=_params(("parallel",)),
        name="inproj",
    )(xs, mods, nw.reshape(1, D_MODEL), w_main, w_ba, w_bat, cos_t, sin_t)


def _ret_kernel(lg_ref, q_ref, k_ref, v_ref, o_ref, sf_ref, sb_ref, sball_ref, *, nc, n, c):
    ph = pl.program_id(2)
    t = pl.program_id(3)
    lg = lg_ref[0]
    lg1 = lg[:, :1]
    ii = lax.broadcasted_iota(jnp.int32, (c, 1), 0).astype(F32)
    gamma_c = jnp.exp(lg * float(c))

    def kv_update(s_ref, k_decay_exponent):
        k = k_ref[...].astype(F32) * jnp.exp(lg * k_decay_exponent)
        kv = _dot(k.T.astype(BF16), v_ref[...])
        s_ref[...] = s_ref[...] * gamma_c[:, :1] + kv

    @pl.when(ph == 0)
    def _():
        @pl.when(t == 0)
        def _():
            sb_ref[...] = jnp.zeros_like(sb_ref)
        pos = jnp.where(t < nc, nc - 1 - t, nc + n - 1 - (t - nc))
        sball_ref[pos] = sb_ref[...]
        kv_update(sb_ref, ii)

    @pl.when(ph == 1)
    def _():
        @pl.when(t == 0)
        def _():
            sf_ref[...] = jnp.zeros_like(sf_ref)
        q = q_ref[...]
        qf = q.astype(F32)
        jj = lax.broadcasted_iota(jnp.int32, (1, c), 1).astype(F32)
        dm = jnp.exp(lg1 * jnp.abs(ii - jj))
        s = _dot_nt(q, k_ref[...]) * dm
        o = _dot(s.astype(BF16), v_ref[...])
        o = o + _dot((qf * jnp.exp(lg * (ii + 1.0))).astype(BF16), sf_ref[...].astype(BF16))
        o = o + _dot((qf * jnp.exp(lg * (float(c) - ii))).astype(BF16), sball_ref[t].astype(BF16))
        o = o * lax.rsqrt(jnp.mean(o * o, axis=-1, keepdims=True) + EPS)
        o_ref[...] = o.astype(BF16)
        kv_update(sf_ref, float(c) - 1.0 - ii)


def _retention(rq, rk, rv, log_gamma_t, batch, l, lc):
    nt = rq.shape[0]
    c = min(256, lc)
    nc, n = lc // c, l // c
    lat0 = lambda b: b * n
    ctx0 = lambda b: batch * n + b * nc

    def chunk(b, ph, t):
        fwd = jnp.where(t < nc, ctx0(b) + t, lat0(b) + t - nc)
        bwd = jnp.where(t < nc, ctx0(b) + nc - 1 - t, lat0(b) + n - 1 - (t - nc))
        return jnp.where(ph == 0, bwd, fwd)

    def q_chunk(b, ph, t):
        return jnp.where(ph == 0, ctx0(b), chunk(b, ph, t))

    return pl.pallas_call(
        functools.partial(_ret_kernel, nc=nc, n=n, c=c),
        grid=(batch, HEADS, 2, nc + n),
        in_specs=[
            pl.BlockSpec((1, 1, LANES), lambda b, h, ph, t: (h, 0, 0)),
            pl.BlockSpec((c, DK), lambda b, h, ph, t: (q_chunk(b, ph, t), h)),
            pl.BlockSpec((c, DK), lambda b, h, ph, t: (chunk(b, ph, t), h)),
            pl.BlockSpec((c, DV), lambda b, h, ph, t: (chunk(b, ph, t), h)),
        ],
        out_specs=pl.BlockSpec((c, DV), lambda b, h, ph, t: (q_chunk(b, ph, t), h)),
        out_shape=jax.ShapeDtypeStruct((nt, VW), BF16),
        scratch_shapes=[
            pltpu.VMEM((DK, DV), F32),
            pltpu.VMEM((DK, DV), F32),
            pltpu.VMEM((nc + n, DK, DV), F32),
        ],
        compiler_params=_params(("parallel", "parallel", "arbitrary", "arbitrary")),
        name="retention",
    )(log_gamma_t, rq, rk, rv)


def _mm3(a, b):
    ah = a.astype(BF16)
    al = (a - ah.astype(F32)).astype(BF16)
    bh = b.astype(BF16)
    bl = (b - bh.astype(F32)).astype(BF16)
    return _dot(ah, bh) + _dot(ah, bl) + _dot(al, bh)


TRI_BASE = 8


def _tri_masks(ri, ci, c):
    diag = (ri // TRI_BASE) == (ci // TRI_BASE)
    offs = []
    s = TRI_BASE
    while s < c:
        offs.append(((ri // s) ^ (ci // s)) == 1)
        s *= 2
    return diag, offs


def _unit_tri_inverse(a, eye, masks):
    diag, offs = masks
    ad = jnp.where(diag, a, 0.0)
    t = eye - ad
    q = ad
    for _ in range(int(math.log2(TRI_BASE)) - 1):
        q = _mm3(q, q)
        t = t + _mm3(t, q)
    for off in offs:
        t = t - _mm3(_mm3(t, jnp.where(off, a, 0.0)), t)
    return t


def _softplus(x):
    return jnp.maximum(x, 0.0) + jnp.log1p(jnp.exp(-jnp.abs(x)))


def _dnprep_kernel(main_ref, prev_ref, next_ref, ba_ref, bat_ref, cw_ref, arow_ref, acol_ref, dtrow_ref, dtcol_ref,
                   p1_ref, p2_ref, u_ref, eg_ref, ext_ref, *, lat_chunks, ctx_chunks, n_lat):
    c = DN_CHUNK
    g = pl.program_id(0)
    in_seq = jnp.where(g < n_lat, g % lat_chunks, (g - n_lat) % ctx_chunks)
    seq_len = jnp.where(g < n_lat, lat_chunks, ctx_chunks)
    first = in_seq == 0
    last = in_seq == seq_len - 1

    ext_ref[0:SUBLANES, :] = jnp.where(first, 0.0, prev_ref[...])
    ext_ref[SUBLANES:SUBLANES + c, :] = main_ref[...]
    ext_ref[SUBLANES + c:2 * SUBLANES + c, :] = jnp.where(last, 0.0, next_ref[...])
    pad = DN_CONV // 2
    acc = None
    for k in range(DN_CONV):
        term = ext_ref[pl.ds(SUBLANES - pad + k, c), :] * cw_ref[k:k + 1, :]
        acc = term if acc is None else acc + term
    y = _silu(acc)

    ri = lax.broadcasted_iota(jnp.int32, (c, c), 0)
    ci = lax.broadcasted_iota(jnp.int32, (c, c), 1)
    lower = ri >= ci
    upper = ri <= ci
    eye = (ri == ci).astype(F32)
    ones_lo = lower.astype(F32)
    ones_up = upper.astype(F32)
    masks = _tri_masks(ri, ci, c)
    hp = lax.Precision.HIGHEST
    ba = ba_ref[...]
    sig = jax.nn.sigmoid(ba)
    g_col = -jnp.exp(arow_ref[...]) * _softplus(ba + dtrow_ref[...])
    gc_col_f = jnp.dot(ones_lo, g_col, preferred_element_type=F32, precision=hp)
    gc_col_b = jnp.dot(ones_up, g_col, preferred_element_type=F32, precision=hp)
    bat = bat_ref[...]
    g_row = -jnp.exp(acol_ref[...]) * _softplus(bat + dtcol_ref[...])
    gc_row_f = jnp.dot(g_row, ones_up, preferred_element_type=F32, precision=hp)
    gc_row_b = jnp.dot(g_row, ones_lo, preferred_element_type=F32, precision=hp)

    for h in range(HEADS):
        q = y[:, h * DK:(h + 1) * DK]
        k = y[:, QK + h * DK:QK + (h + 1) * DK]
        v = y[:, 2 * QK + h * DV:2 * QK + (h + 1) * DV]
        q = q * lax.rsqrt(jnp.sum(q * q, axis=-1, keepdims=True) + EPS) * (DK ** -0.5)
        k = k * lax.rsqrt(jnp.sum(k * k, axis=-1, keepdims=True) + EPS)
        kb16 = k.astype(BF16)
        kk = _dot_nt(kb16, kb16)
        qk = _dot_nt(q.astype(BF16), kb16)
        k_t = k.T
        for d in range(2):
            hd = d * HEADS + h
            lane = 2 * HEADS + hd
            gcol = (gc_col_f if d == 0 else gc_col_b)[:, lane:lane + 1]
            grow = (gc_row_f if d == 0 else gc_row_b)[lane:lane + 1, :]
            incl = lower if d == 0 else upper
            strict = (ri > ci) if d == 0 else (ri < ci)
            decay = jnp.where(incl, jnp.exp(jnp.where(incl, gcol - grow, 0.0)), 0.0)
            beta = sig[:, hd:hd + 1]
            a = jnp.where(strict, beta * kk * decay, 0.0)
            tmat = _unit_tri_inverse(a, eye, masks)
            egc = jnp.exp(gcol)
            rhs = jnp.concatenate([v * beta, k * (beta * egc)], axis=1).astype(BF16)
            uw = _dot(tmat.astype(BF16), rhs)
            glast = grow[:, c - 1:c] if d == 0 else grow[:, 0:1]
            u_ref[0, hd] = uw[:, :DV].astype(BF16)
            p1_ref[0, hd, 0:c, :] = uw[:, DV:].astype(BF16)
            p1_ref[0, hd, c:2 * c, :] = (q * egc).astype(BF16)
            p2_ref[0, hd, 0:c, :] = (qk * decay).astype(BF16)
            p2_ref[0, hd, c:2 * c, :] = (k_t * jnp.exp(glast - grow)).astype(BF16)
            eg_ref[0, hd] = jnp.broadcast_to(jnp.exp(glast), (1, DV))


def _dnprep(dpre, ba, bat, conv_w, a_log, dt_bias, batch, l, lc):
    nt = dpre.shape[0]
    c = DN_CHUNK
    nch = nt // c
    per = c // SUBLANES
    nblk8 = nt // SUBLANES
    flat_a = jnp.concatenate([jnp.zeros((2 * HEADS,), F32), a_log.reshape(-1).astype(F32)])
    flat_dt = jnp.concatenate([jnp.zeros((2 * HEADS,), F32), dt_bias.reshape(-1).astype(F32)])
    arow = jnp.pad(flat_a, (0, LANES - N_BA)).reshape(1, LANES)
    dtrow = jnp.pad(flat_dt, (0, LANES - N_BA)).reshape(1, LANES)
    acol = flat_a.reshape(N_BA, 1)
    dtcol = flat_dt.reshape(N_BA, 1)
    hd = 2 * HEADS
    return pl.pallas_call(
        functools.partial(_dnprep_kernel, lat_chunks=l // c, ctx_chunks=lc // c, n_lat=batch * l // c),
        grid=(nch,),
        in_specs=[
            pl.BlockSpec((c, CONV_CH), lambda g: (g, 0)),
            pl.BlockSpec((SUBLANES, CONV_CH), lambda g: (jnp.maximum(g * per - 1, 0), 0)),
            pl.BlockSpec((SUBLANES, CONV_CH), lambda g: (jnp.minimum((g + 1) * per, nblk8 - 1), 0)),
            pl.BlockSpec((c, LANES), lambda g: (g, 0)),
            pl.BlockSpec((N_BA, c), lambda g: (0, g)),
            _const_spec((DN_CONV, CONV_CH)),
            _const_spec((1, LANES)),
            _const_spec((N_BA, 1)),
            _const_spec((1, LANES)),
            _const_spec((N_BA, 1)),
        ],
        out_specs=[
            pl.BlockSpec((1, hd, 2 * c, DK), lambda g: (g, 0, 0, 0)),
            pl.BlockSpec((1, hd, 2 * c, c), lambda g: (g, 0, 0, 0)),
            pl.BlockSpec((1, hd, c, DV), lambda g: (g, 0, 0, 0)),
            pl.BlockSpec((1, hd, 1, DV), lambda g: (g, 0, 0, 0)),
        ],
        out_shape=[
            jax.ShapeDtypeStruct((nch, hd, 2 * c, DK), BF16),
            jax.ShapeDtypeStruct((nch, hd, 2 * c, c), BF16),
            jax.ShapeDtypeStruct((nch, hd, c, DV), BF16),
            jax.ShapeDtypeStruct((nch, hd, 1, DV), F32),
        ],
        scratch_shapes=[pltpu.VMEM((c + 2 * SUBLANES, CONV_CH), F32)],
        compiler_params=_params(("parallel",)),
        name="dnprep",
    )(dpre, dpre, dpre, ba, bat, conv_w.astype(F32), arow, acol, dtrow, dtcol)


def _dnseq_kernel(p1f_ref, p2f_ref, uf_ref, egf_ref, p1b_ref, p2b_ref, ub_ref, egb_ref,
                  of_ref, ob_ref, s_ref):
    c = DN_CHUNK

    @pl.when(pl.program_id(1) == 0)
    def _():
        s_ref[...] = jnp.zeros_like(s_ref)

    dirs = ((p1f_ref, p2f_ref, uf_ref, egf_ref, of_ref), (p1b_ref, p2b_ref, ub_ref, egb_ref, ob_ref))
    for d, (p1_ref, p2_ref, u_ref, eg_ref, o_ref) in enumerate(dirs):
        for h in range(HEADS):
            s = s_ref[d * HEADS + h]
            r = _dot(p1_ref[0, h], s.astype(BF16))
            vnew = u_ref[0, h].astype(F32) - r[:c]
            r2 = _dot(p2_ref[0, h], vnew.astype(BF16))
            o_ref[:, h * DV:(h + 1) * DV] = r[c:] + r2[:c]
            s_ref[d * HEADS + h] = s * eg_ref[0, h] + r2[c:]


def _dnseq(p1, p2, u, eg, batch, l, lc):
    c = DN_CHUNK
    nch = p1.shape[0]
    nt = nch * c
    nc, n = lc // c, l // c

    def fwd(b, t):
        return jnp.where(t < nc, batch * n + b * nc + t, b * n + t - nc)

    def bwd(b, t):
        return jnp.where(t < nc, batch * n + b * nc + nc - 1 - t, b * n + n - 1 - (t - nc))

    in_specs, args = [], []
    for d, cm in enumerate((fwd, bwd)):
        for arr in (p1, p2, u, eg):
            shp = (1, HEADS) + arr.shape[2:]
            in_specs.append(pl.BlockSpec(shp, functools.partial(lambda b, t, cm, d: (cm(b, t), d, 0, 0), cm=cm, d=d)))
            args.append(arr)
    return pl.pallas_call(
        _dnseq_kernel,
        grid=(batch, nc + n),
        in_specs=in_specs,
        out_specs=[
            pl.BlockSpec((c, VW), lambda b, t: (fwd(b, t), 0)),
            pl.BlockSpec((c, VW), lambda b, t: (bwd(b, t), 0)),
        ],
        out_shape=[jax.ShapeDtypeStruct((nt, VW), F32)] * 2,
        scratch_shapes=[pltpu.VMEM((2 * HEADS, DK, DV), F32)],
        compiler_params=_params(("parallel", "arbitrary")),
        name="dnseq",
    )(*args)


def _out_kernel(x_ref, m_ref, ret_ref, srg_ref, sdz_ref, sga_ref, sgb_ref, dnw_ref, wr_ref, wdn_ref, wo_ref,
                of_ref, ob_ref, out_ref):
    x = x_ref[...]
    gate = m_ref[0, 5:6, :]
    ret = (ret_ref[...].astype(F32) * srg_ref[...].astype(F32)).astype(BF16)
    yr = _dot(ret, wr_ref[...])
    dn_o = of_ref[...] + ob_ref[...]
    dnw = dnw_ref[...]
    slabs = []
    for h in range(HEADS):
        o = dn_o[:, h * DV:(h + 1) * DV]
        slabs.append(o * lax.rsqrt(jnp.mean(o * o, axis=-1, keepdims=True) + EPS) * dnw)
    dn = (jnp.concatenate(slabs, axis=1) * sdz_ref[...].astype(F32)).astype(BF16)
    yd = _dot(dn, wdn_ref[...])
    y = sga_ref[...].astype(F32) * yr + sgb_ref[...].astype(F32) * yd
    out_ref[...] = x + gate * _dot(y.astype(BF16), wo_ref[...])


def _mixer_out(xs, mods, ret, srg, sdz, sga, sgb, dn_norm_w, w_ret, w_dn, w_o, o_f, o_b, tm, tiles_per_batch, batch):
    nt = xs.shape[0]
    row = lambda w: pl.BlockSpec((tm, w), lambda i: (i, 0))
    return pl.pallas_call(
        _out_kernel,
        grid=(nt // tm,),
        in_specs=[
            row(D_MODEL), _mod_spec(tiles_per_batch, batch),
            row(VW), row(VW), row(VW), row(D_MODEL), row(D_MODEL),
            _const_spec((1, DV)),
            _const_spec((VW, D_MODEL)), _const_spec((VW, D_MODEL)), _const_spec((D_MODEL, D_MODEL)),
            row(VW), row(VW),
        ],
        out_specs=row(D_MODEL),
        out_shape=jax.ShapeDtypeStruct((nt, D_MODEL), F32),
        compiler_params=_params(("parallel",)),
        name="mixer_out",
    )(xs, mods, ret, srg, sdz, sga, sgb, dn_norm_w.reshape(1, DV).astype(F32), w_ret, w_dn, w_o, o_f, o_b)


def _final_norm_kernel(x_ref, w_ref, o_ref):
    x = x_ref[...]
    o_ref[...] = x * lax.rsqrt(jnp.mean(x * x, axis=-1, keepdims=True) + EPS) * w_ref[...]


def _final_norm(xs, w, rows, tm):
    return pl.pallas_call(
        _final_norm_kernel,
        grid=(rows // tm,),
        in_specs=[pl.BlockSpec((tm, D_MODEL), lambda i: (i, 0)), _const_spec((1, D_MODEL))],
        out_specs=pl.BlockSpec((tm, D_MODEL), lambda i: (i, 0)),
        out_shape=jax.ShapeDtypeStruct((rows, D_MODEL), F32),
        compiler_params=_params(("parallel",)),
        name="final_norm",
    )(xs, w.reshape(1, D_MODEL))


def _rope_tables(l, tm):
    half = DK // 2
    n_freq = half // 2
    inv = ROPE_BASE ** (-jnp.arange(n_freq, dtype=F32) / n_freq)
    rows = l // GRID_W
    ang_r = jnp.arange(rows, dtype=F32)[:, None] * inv
    ang_c = jnp.arange(GRID_W, dtype=F32)[:, None] * inv
    ang = jnp.concatenate([
        jnp.broadcast_to(ang_r[:, None, :], (rows, GRID_W, n_freq)),
        jnp.broadcast_to(ang_c[None, :, :], (rows, GRID_W, n_freq)),
    ], axis=-1).reshape(l, half)
    cos, sin = jnp.cos(ang), jnp.sin(ang)
    cos_t = jnp.concatenate([jnp.concatenate([cos, cos], axis=1), jnp.ones((tm, DK), F32)], axis=0)
    sin_t = jnp.concatenate([jnp.concatenate([-sin, sin], axis=1), jnp.zeros((tm, DK), F32)], axis=0)
    return cos_t, sin_t


def kernel(x, c, ctx, c_ctx, ada_w, ada_b, norm_w, ffn1_wgu, ffn1_wd, w_in, dn_conv_w, dn_a_log, dn_dt_bias,
           dn_norm_w, w_ret_out, w_dn_out, w_o, ffn2_wgu, ffn2_wd, final_norm_w):
    batch, l, _ = x.shape
    lc = ctx.shape[1]
    depth = ada_w.shape[0]
    assert batch + 1 <= SUBLANES and l % DN_CHUNK == 0 and lc % DN_CHUNK == 0 and l % GRID_W == 0
    tm = _row_tile(l, batch * lc)
    tiles_per_batch = l // tm

    xs = jnp.concatenate([x.reshape(batch * l, D_MODEL), ctx.reshape(batch * lc, D_MODEL)], axis=0)
    cmat = jnp.zeros((SUBLANES, D_MODEL), F32).at[:batch].set(c).at[batch].set(c_ctx)
    mods_all = _ada_all(cmat, ada_w, ada_b).reshape(depth, SUBLANES, N_MOD, D_MODEL)
    cos_t, sin_t = _rope_tables(l, tm)
    log_gamma = jnp.log1p(-jnp.power(2.0, -5.0 - jnp.arange(HEADS, dtype=F32)))
    log_gamma_t = jnp.broadcast_to(log_gamma[:, None, None], (HEADS, 1, LANES))

    for layer in range(depth):
        mods = mods_all[layer, :batch + 1]
        w_l = w_in[layer]
        w_main = jnp.concatenate([w_l[:, :_OFF_GA], w_l[:, _OFF_GA + N_BA:]], axis=1).astype(BF16)
        w_ba_cols = w_l[:, _OFF_GA:_OFF_GA + N_BA]
        w_ba = jnp.pad(w_ba_cols, ((0, 0), (0, LANES - N_BA))).astype(BF16)
        w_bat = w_ba_cols.T.astype(BF16)

        xs = _ffn(xs, mods, norm_w[layer, 0], ffn1_wgu[layer].astype(BF16), ffn1_wd[layer].astype(BF16),
                  0, tm, tiles_per_batch, batch)
        rq, rk, rv, srg, dpre, sdz, ba, bat, sga, sgb = _inproj(
            xs, mods, norm_w[layer, 1], w_main, w_ba, w_bat, cos_t, sin_t, tm, tiles_per_batch, batch)
        ret = _retention(rq, rk, rv, log_gamma_t, batch, l, lc)
        p1, p2, u, eg = _dnprep(dpre, ba, bat, dn_conv_w[layer], dn_a_log[layer], dn_dt_bias[layer], batch, l, lc)
        o_f, o_b = _dnseq(p1, p2, u, eg, batch, l, lc)
        xs = _mixer_out(xs, mods, ret, srg, sdz, sga, sgb, dn_norm_w[layer], w_ret_out[layer].astype(BF16),
                        w_dn_out[layer].astype(BF16), w_o[layer].astype(BF16), o_f, o_b, tm, tiles_per_batch, batch)
        xs = _ffn(xs, mods, norm_w[layer, 2], ffn2_wgu[layer].astype(BF16), ffn2_wd[layer].astype(BF16),
                  6, tm, tiles_per_batch, batch)

    out = _final_norm(xs, final_norm_w, batch * l, tm)
    return out.reshape(batch, l, D_MODEL)
```

```python
import functools
import math

import jax
import jax.numpy as jnp
from jax import lax
from jax.experimental import pallas as pl
from jax.experimental.pallas import tpu as pltpu

F32 = jnp.float32
BF16 = jnp.bfloat16

D_MODEL = 1024
FFN_HIDDEN = 2816
N_MOD = 9
GRID_W = 64
ROPE_BASE = 10000.0
EPS = 1e-6

HEADS = 4
DK = 128
DV = 256
QK = HEADS * DK
VW = HEADS * DV
DN_CONV = 5
CONV_CH = 2 * QK + VW
N_BA = 4 * HEADS

LANES = 128
SUBLANES = 8
MXU_N = 256
VMEM_LIMIT = 56 * 1024 * 1024

DN_CHUNK = 128
FFN_CHUNK = MXU_N


def _dot(a, b):
    return jnp.dot(a, b, preferred_element_type=F32)


def _dot_nt(a, b):
    return lax.dot_general(a, b, (((1,), (1,)), ((), ())), preferred_element_type=F32)


def _silu(x):
    return x * jax.nn.sigmoid(x)


def _row_tile(l, lc_total):
    for t in (512, 256, 128):
        if l % t == 0 and lc_total % t == 0:
            return t
    raise ValueError("sequence lengths must be multiples of 128")


def _params(sem):
    return pltpu.CompilerParams(dimension_semantics=sem, vmem_limit_bytes=VMEM_LIMIT)


def _const_spec(shape):
    nd = len(shape)
    return pl.BlockSpec(shape, lambda *_: (0,) * nd, pipeline_mode=pl.Buffered(1))


def _ada_kernel(c_ref, w_ref, b_ref, o_ref):
    c = c_ref[...]
    s = _silu(c)
    o_ref[0] = jnp.dot(s, w_ref[0], preferred_element_type=F32,
                       precision=lax.Precision.HIGHEST) + b_ref[0]


def _ada_all(cmat, ada_w, ada_b):
    depth = ada_w.shape[0]
    n = ada_w.shape[2]
    tn = 1024
    return pl.pallas_call(
        _ada_kernel,
        grid=(depth, n // tn),
        in_specs=[
            pl.BlockSpec((SUBLANES, D_MODEL), lambda l, j: (0, 0)),
            pl.BlockSpec((1, D_MODEL, tn), lambda l, j: (l, 0, j)),
            pl.BlockSpec((1, 1, tn), lambda l, j: (l, 0, j)),
        ],
        out_specs=pl.BlockSpec((1, SUBLANES, tn), lambda l, j: (l, 0, j)),
        out_shape=jax.ShapeDtypeStruct((depth, SUBLANES, n), F32),
        compiler_params=_params(("parallel", "parallel")),
        name="ada",
    )(cmat, ada_w, ada_b.reshape(depth, 1, n))


def _norm_mod(x, nw, shift, scale):
    ms = jnp.mean(x * x, axis=-1, keepdims=True)
    h = x * lax.rsqrt(ms + EPS) * nw
    return h * (1.0 + scale) + shift


def _mod_spec(tiles_per_batch, batch):
    return pl.BlockSpec((1, N_MOD, D_MODEL), lambda i: (jnp.minimum(i // tiles_per_batch, batch), 0, 0))


def _ffn_kernel(x_ref, m_ref, nw_ref, wgu_ref, wd_ref, fw_ref, o_ref, *, k0, final):
    x = x_ref[...]
    shift = m_ref[0, k0:k0 + 1, :]
    scale = m_ref[0, k0 + 1:k0 + 2, :]
    gate = m_ref[0, k0 + 2:k0 + 3, :]
    h = _norm_mod(x, nw_ref[...], shift, scale).astype(BF16)
    acc = jnp.zeros(x.shape, F32)
    for c in range(FFN_HIDDEN // FFN_CHUNK):
        lo = c * FFN_CHUNK
        g = _dot(h, wgu_ref[:, lo:lo + FFN_CHUNK])
        u = _dot(h, wgu_ref[:, FFN_HIDDEN + lo:FFN_HIDDEN + lo + FFN_CHUNK])
        a = (_silu(g) * u).astype(BF16)
        acc = acc + _dot(a, wd_ref[lo:lo + FFN_CHUNK, :])
    out = x + (0.5 * gate) * acc
    if final:
        out = out * lax.rsqrt(jnp.mean(out * out, axis=-1, keepdims=True) + EPS) * fw_ref[...]
    o_ref[...] = out


def _ffn(xs, mods, nw, wgu, wd, final_w, k0, tm, tiles_per_batch, batch, final=False):
    rows = tiles_per_batch * batch * tm if final else xs.shape[0]
    return pl.pallas_call(
        functools.partial(_ffn_kernel, k0=k0, final=final),
        grid=(rows // tm,),
        in_specs=[
            pl.BlockSpec((tm, D_MODEL), lambda i: (i, 0)),
            _mod_spec(tiles_per_batch, batch),
            _const_spec((1, D_MODEL)),
            _const_spec((D_MODEL, 2 * FFN_HIDDEN)),
            _const_spec((FFN_HIDDEN, D_MODEL)),
            _const_spec((1, D_MODEL)),
        ],
        out_specs=pl.BlockSpec((tm, D_MODEL), lambda i: (i, 0)),
        out_shape=jax.ShapeDtypeStruct((rows, D_MODEL), F32),
        compiler_params=_params(("parallel",)),
        name="ffn",
    )(xs, mods, nw.reshape(1, D_MODEL), wgu, wd, final_w.reshape(1, D_MODEL))


_OFF_RQ, _OFF_RK, _OFF_RV, _OFF_RG = 0, QK, 2 * QK, 2 * QK + VW
_OFF_DQKV = 2 * QK + 2 * VW
_OFF_DZ = _OFF_DQKV + CONV_CH
_OFF_GA = _OFF_DZ + VW
_OFF_GB = _OFF_GA + D_MODEL
_W_MAIN = _OFF_GB + D_MODEL


def _rope(p, cos, sin):
    outs = []
    for hh in range(HEADS):
        slab = p[:, hh * DK:(hh + 1) * DK]
        outs.append(slab * cos + pltpu.roll(slab, DK // 2, 1) * sin)
    return jnp.concatenate(outs, axis=1)


def _inproj_kernel(x_ref, m_ref, nw_ref, w_ref, wba_ref, wbat_ref, cos_ref, sin_ref,
                   rq_ref, rk_ref, rv_ref, srg_ref, dpre_ref, sdz_ref, ba_ref, bat_ref, sga_ref, sgb_ref):
    x = x_ref[...]
    h = _norm_mod(x, nw_ref[...], m_ref[0, 3:4, :], m_ref[0, 4:5, :]).astype(BF16)
    cos = cos_ref[...]
    sin = sin_ref[...]
    rq_ref[...] = _rope(_dot(h, w_ref[:, _OFF_RQ:_OFF_RQ + QK]), cos, sin).astype(BF16)
    rk_ref[...] = _rope(_dot(h, w_ref[:, _OFF_RK:_OFF_RK + QK]) * (DK ** -0.5), cos, sin).astype(BF16)
    rv_ref[...] = _dot(h, w_ref[:, _OFF_RV:_OFF_RV + VW]).astype(BF16)
    srg_ref[...] = _silu(_dot(h, w_ref[:, _OFF_RG:_OFF_RG + VW])).astype(BF16)
    half = CONV_CH // 2
    dpre_ref[:, :half] = _dot(h, w_ref[:, _OFF_DQKV:_OFF_DQKV + half])
    dpre_ref[:, half:] = _dot(h, w_ref[:, _OFF_DQKV + half:_OFF_DQKV + CONV_CH])
    sdz_ref[...] = _silu(_dot(h, w_ref[:, _OFF_DZ:_OFF_DZ + VW])).astype(BF16)
    sga_ref[...] = jax.nn.sigmoid(_dot(h, w_ref[:, _OFF_GA:_OFF_GA + D_MODEL])).astype(BF16)
    sgb_ref[...] = jax.nn.sigmoid(_dot(h, w_ref[:, _OFF_GB:_OFF_GB + D_MODEL])).astype(BF16)
    ba_ref[...] = _dot(h, wba_ref[...])
    bat_ref[...] = _dot_nt(wbat_ref[...], h)


def _inproj(xs, mods, nw, w_main, w_ba, w_bat, cos_t, sin_t, tm, tiles_per_batch, batch):
    nt = xs.shape[0]
    n_lat_tiles = tiles_per_batch * batch

    def rope_map(i):
        return (jnp.where(i < n_lat_tiles, i % tiles_per_batch, tiles_per_batch), 0)

    row = lambda w: pl.BlockSpec((tm, w), lambda i: (i, 0))
    outs = [
        (QK, BF16), (QK, BF16), (VW, BF16), (VW, BF16), (CONV_CH, F32), (VW, BF16),
        (LANES, F32), None, (D_MODEL, BF16), (D_MODEL, BF16),
    ]
    out_specs, out_shape = [], []
    for o in outs:
        if o is None:
            out_specs.append(pl.BlockSpec((N_BA, tm), lambda i: (0, i)))
            out_shape.append(jax.ShapeDtypeStruct((N_BA, nt), F32))
        else:
            out_specs.append(row(o[0]))
            out_shape.append(jax.ShapeDtypeStruct((nt, o[0]), o[1]))
    return pl.pallas_call(
        _inproj_kernel,
        grid=(nt // tm,),
        in_specs=[
            row(D_MODEL),
            _mod_spec(tiles_per_batch, batch),
            _const_spec((1, D_MODEL)),
            _const_spec((D_MODEL, _W_MAIN)),
            _const_spec((D_MODEL, LANES)),
            _const_spec((N_BA, D_MODEL)),
            pl.BlockSpec((tm, DK), rope_map),
            pl.BlockSpec((tm, DK), rope_map),
        ],
        out_specs=out_specs,
        out_shape=out_shape,
        compiler_params=_params(("parallel",)),
        name="inproj",
    )(xs, mods, nw.reshape(1, D_MODEL), w_main, w_ba, w_bat, cos_t, sin_t)


def _ret_kernel(lg_ref, q_ref, k_ref, v_ref, o_ref,
                sf_ref, sb_ref, sball_ref, dm_ref, eq_ref, ekf_ref, ekb_ref, *, nc, n, c):
    ph = pl.program_id(1)
    t = pl.program_id(2)

    @pl.when((ph == 0) & (t == 0))
    def _():
        sb_ref[...] = jnp.zeros_like(sb_ref)
        ii = lax.broadcasted_iota(jnp.int32, (c, 1), 0).astype(F32)
        jj = lax.broadcasted_iota(jnp.int32, (1, c), 1).astype(F32)
        for h in range(HEADS):
            lg = lg_ref[h]
            dm_ref[h] = jnp.exp(lg[:, :1] * jnp.abs(ii - jj))
            eq_ref[h, :, :DK] = jnp.exp(lg * (ii + 1.0))
            eq_ref[h, :, DK:] = jnp.exp(lg * (float(c) - ii))
            ekf_ref[h] = jnp.exp(lg * (float(c) - 1.0 - ii))
            ekb_ref[h] = jnp.exp(lg * ii)

    def kv_update(h, s_ref, ek_ref):
        k = k_ref[:, h * DK:(h + 1) * DK].astype(F32) * ek_ref[h]
        kv = _dot(k.T.astype(BF16), v_ref[:, h * DV:(h + 1) * DV])
        gamma_c = jnp.exp(lg_ref[h][:, :1] * float(c))
        s_ref[h] = s_ref[h] * gamma_c + kv

    @pl.when(ph == 0)
    def _():
        pos = jnp.where(t < nc, nc - 1 - t, nc + n - 1 - (t - nc))
        for h in range(HEADS):
            sball_ref[pos, h] = sb_ref[h].astype(BF16)
            kv_update(h, sb_ref, ekb_ref)

    @pl.when(ph == 1)
    def _():
        @pl.when(t == 0)
        def _():
            sf_ref[...] = jnp.zeros_like(sf_ref)
        for h in range(HEADS):
            q = q_ref[:, h * DK:(h + 1) * DK]
            qf = q.astype(F32)
            s = _dot_nt(q, k_ref[:, h * DK:(h + 1) * DK]) * dm_ref[h]
            o = _dot(s.astype(BF16), v_ref[:, h * DV:(h + 1) * DV])
            q2 = (jnp.concatenate([qf, qf], axis=1) * eq_ref[h]).astype(BF16)
            s2 = jnp.concatenate([sf_ref[h].astype(BF16), sball_ref[t, h]], axis=0)
            o = o + _dot(q2, s2)
            o = o * lax.rsqrt(jnp.mean(o * o, axis=-1, keepdims=True) + EPS)
            o_ref[:, h * DV:(h + 1) * DV] = o.astype(BF16)
            kv_update(h, sf_ref, ekf_ref)


def _retention(rq, rk, rv, log_gamma_t, batch, l, lc):
    nt = rq.shape[0]
    c = min(256, lc)
    nc, n = lc // c, l // c
    lat0 = lambda b: b * n
    ctx0 = lambda b: batch * n + b * nc

    def chunk(b, ph, t):
        fwd = jnp.where(t < nc, ctx0(b) + t, lat0(b) + t - nc)
        bwd = jnp.where(t < nc, ctx0(b) + nc - 1 - t, lat0(b) + n - 1 - (t - nc))
        return jnp.where(ph == 0, bwd, fwd)

    def q_chunk(b, ph, t):
        return jnp.where(ph == 0, ctx0(b), chunk(b, ph, t))

    return pl.pallas_call(
        functools.partial(_ret_kernel, nc=nc, n=n, c=c),
        grid=(batch, 2, nc + n),
        in_specs=[
            _const_spec((HEADS, 1, LANES)),
            pl.BlockSpec((c, QK), lambda b, ph, t: (q_chunk(b, ph, t), 0)),
            pl.BlockSpec((c, QK), lambda b, ph, t: (chunk(b, ph, t), 0)),
            pl.BlockSpec((c, VW), lambda b, ph, t: (chunk(b, ph, t), 0)),
        ],
        out_specs=pl.BlockSpec((c, VW), lambda b, ph, t: (q_chunk(b, ph, t), 0)),
        out_shape=jax.ShapeDtypeStruct((nt, VW), BF16),
        scratch_shapes=[
            pltpu.VMEM((HEADS, DK, DV), F32),
            pltpu.VMEM((HEADS, DK, DV), F32),
            pltpu.VMEM((nc + n, HEADS, DK, DV), BF16),
            pltpu.VMEM((HEADS, c, c), F32),
            pltpu.VMEM((HEADS, c, 2 * DK), F32),
            pltpu.VMEM((HEADS, c, DK), F32),
            pltpu.VMEM((HEADS, c, DK), F32),
        ],
        compiler_params=_params(("parallel", "arbitrary", "arbitrary")),
        name="retention",
    )(log_gamma_t, rq, rk, rv)


def _mm(a, b):
    return _dot(a.astype(BF16), b.astype(BF16))


TRI_BASE = 8


def _tri_masks(ri, ci, c):
    diag = (ri // TRI_BASE) == (ci // TRI_BASE)
    offs = []
    s = TRI_BASE
    while s < c:
        offs.append(((ri // s) ^ (ci // s)) == 1)
        s *= 2
    return diag, offs


def _unit_tri_inverses(mats, eye, masks):
    diag, offs = masks
    qs = [jnp.where(diag, a, 0.0) for a in mats]
    ts = [eye - q for q in qs]
    for _ in range(int(math.log2(TRI_BASE)) - 1):
        qs = [_mm(q, q) for q in qs]
        ts = [t + _mm(t, q) for t, q in zip(ts, qs)]
    for off in offs:
        half = [_mm(t, jnp.where(off, a, 0.0)) for t, a in zip(ts, mats)]
        ts = [t - _mm(hf, t) for t, hf in zip(ts, half)]
    return ts


def _softplus(x):
    return jnp.maximum(x, 0.0) + jnp.log1p(jnp.exp(-jnp.abs(x)))


def _cumsum(x, axis, reverse):
    n = x.shape[axis]
    idx = lax.broadcasted_iota(jnp.int32, x.shape, axis)
    s = 1
    while s < n:
        if reverse:
            x = x + jnp.where(idx < n - s, pltpu.roll(x, n - s, axis), 0.0)
        else:
            x = x + jnp.where(idx >= s, pltpu.roll(x, s, axis), 0.0)
        s *= 2
    return x


def _dnprep_kernel(main_ref, prev_ref, next_ref, ba_ref, bat_ref, cw_ref, arow_ref, acol_ref, dtrow_ref, dtcol_ref,
                   p1_ref, p2_ref, u_ref, eg_ref, ext_ref, *, lat_chunks, ctx_chunks, n_lat):
    c = DN_CHUNK
    g = pl.program_id(0)
    in_seq = jnp.where(g < n_lat, g % lat_chunks, (g - n_lat) % ctx_chunks)
    seq_len = jnp.where(g < n_lat, lat_chunks, ctx_chunks)
    first = in_seq == 0
    last = in_seq == seq_len - 1

    ext_ref[0:SUBLANES, :] = jnp.where(first, 0.0, prev_ref[...])
    ext_ref[SUBLANES:SUBLANES + c, :] = main_ref[...]
    ext_ref[SUBLANES + c:2 * SUBLANES + c, :] = jnp.where(last, 0.0, next_ref[...])
    pad = DN_CONV // 2
    acc = None
    for k in range(DN_CONV):
        term = ext_ref[pl.ds(SUBLANES - pad + k, c), :] * cw_ref[k:k + 1, :]
        acc = term if acc is None else acc + term
    y = _silu(acc)

    ri = lax.broadcasted_iota(jnp.int32, (c, c), 0)
    ci = lax.broadcasted_iota(jnp.int32, (c, c), 1)
    eye = (ri == ci).astype(F32)
    masks = _tri_masks(ri, ci, c)
    ba = ba_ref[...]
    sig = jax.nn.sigmoid(ba)
    g_col = -jnp.exp(arow_ref[...]) * _softplus(ba + dtrow_ref[...])
    gc_col = (_cumsum(g_col, 0, False), _cumsum(g_col, 0, True))
    g_row = -jnp.exp(acol_ref[...]) * _softplus(bat_ref[...] + dtcol_ref[...])
    gc_row = (_cumsum(g_row, 1, False), _cumsum(g_row, 1, True))

    qs, ks, vs, kts, qks = [], [], [], [], []
    items = []
    for h in range(HEADS):
        q = y[:, h * DK:(h + 1) * DK]
        k = y[:, QK + h * DK:QK + (h + 1) * DK]
        q = q * lax.rsqrt(jnp.sum(q * q, axis=-1, keepdims=True) + EPS) * (DK ** -0.5)
        k = k * lax.rsqrt(jnp.sum(k * k, axis=-1, keepdims=True) + EPS)
        kb16 = k.astype(BF16)
        qs.append(q)
        ks.append(k)
        vs.append(y[:, 2 * QK + h * DV:2 * QK + (h + 1) * DV])
        kts.append(k.T)
        qks.append(_dot_nt(q.astype(BF16), kb16))
        kk = _dot_nt(kb16, kb16)
        for d in range(2):
            lane = 2 * HEADS + d * HEADS + h
            gcol = gc_col[d][:, lane:lane + 1]
            grow = gc_row[d][lane:lane + 1, :]
            incl = (ri >= ci) if d == 0 else (ri <= ci)
            strict = (ri > ci) if d == 0 else (ri < ci)
            decay = jnp.where(incl, jnp.exp(jnp.where(incl, gcol - grow, 0.0)), 0.0)
            beta = sig[:, d * HEADS + h:d * HEADS + h + 1]
            a = jnp.where(strict, beta * kk * decay, 0.0)
            items.append((h, d, gcol, grow, decay, beta, a))
    tmats = _unit_tri_inverses([it[6] for it in items], eye, masks)
    for (h, d, gcol, grow, decay, beta, _), tmat in zip(items, tmats):
        hd = d * HEADS + h
        q, k, v = qs[h], ks[h], vs[h]
        egc = jnp.exp(gcol)
        rhs = jnp.concatenate([v * beta, k * (beta * egc)], axis=1).astype(BF16)
        uw = _dot(tmat.astype(BF16), rhs)
        glast = grow[:, c - 1:c] if d == 0 else grow[:, 0:1]
        u_ref[0, hd] = uw[:, :DV].astype(BF16)
        p1_ref[0, hd, 0:c, :] = uw[:, DV:].astype(BF16)
        p1_ref[0, hd, c:2 * c, :] = (q * egc).astype(BF16)
        p2_ref[0, hd, 0:c, :] = (qks[h] * decay).astype(BF16)
        p2_ref[0, hd, c:2 * c, :] = (kts[h] * jnp.exp(glast - grow)).astype(BF16)
        eg_ref[0, hd] = jnp.broadcast_to(jnp.exp(glast), (1, DV))


def _dnprep(dpre, ba, bat, conv_w, a_log, dt_bias, batch, l, lc):
    nt = dpre.shape[0]
    c = DN_CHUNK
    nch = nt // c
    per = c // SUBLANES
    nblk8 = nt // SUBLANES
    flat_a = jnp.concatenate([jnp.zeros((2 * HEADS,), F32), a_log.reshape(-1).astype(F32)])
    flat_dt = jnp.concatenate([jnp.zeros((2 * HEADS,), F32), dt_bias.reshape(-1).astype(F32)])
    arow = jnp.pad(flat_a, (0, LANES - N_BA)).reshape(1, LANES)
    dtrow = jnp.pad(flat_dt, (0, LANES - N_BA)).reshape(1, LANES)
    acol = flat_a.reshape(N_BA, 1)
    dtcol = flat_dt.reshape(N_BA, 1)
    hd = 2 * HEADS
    return pl.pallas_call(
        functools.partial(_dnprep_kernel, lat_chunks=l // c, ctx_chunks=lc // c, n_lat=batch * l // c),
        grid=(nch,),
        in_specs=[
            pl.BlockSpec((c, CONV_CH), lambda g: (g, 0)),
            pl.BlockSpec((SUBLANES, CONV_CH), lambda g: (jnp.maximum(g * per - 1, 0), 0)),
            pl.BlockSpec((SUBLANES, CONV_CH), lambda g: (jnp.minimum((g + 1) * per, nblk8 - 1), 0)),
            pl.BlockSpec((c, LANES), lambda g: (g, 0)),
            pl.BlockSpec((N_BA, c), lambda g: (0, g)),
            _const_spec((DN_CONV, CONV_CH)),
            _const_spec((1, LANES)),
            _const_spec((N_BA, 1)),
            _const_spec((1, LANES)),
            _const_spec((N_BA, 1)),
        ],
        out_specs=[
            pl.BlockSpec((1, hd, 2 * c, DK), lambda g: (g, 0, 0, 0)),
            pl.BlockSpec((1, hd, 2 * c, c), lambda g: (g, 0, 0, 0)),
            pl.BlockSpec((1, hd, c, DV), lambda g: (g, 0, 0, 0)),
            pl.BlockSpec((1, hd, 1, DV), lambda g: (g, 0, 0, 0)),
        ],
        out_shape=[
            jax.ShapeDtypeStruct((nch, hd, 2 * c, DK), BF16),
            jax.ShapeDtypeStruct((nch, hd, 2 * c, c), BF16),
            jax.ShapeDtypeStruct((nch, hd, c, DV), BF16),
            jax.ShapeDtypeStruct((nch, hd, 1, DV), F32),
        ],
        scratch_shapes=[pltpu.VMEM((c + 2 * SUBLANES, CONV_CH), F32)],
        compiler_params=_params(("parallel",)),
        name="dnprep",
    )(dpre, dpre, dpre, ba, bat, conv_w.astype(F32), arow, acol, dtrow, dtcol)


def _dnseq_kernel(p1f_ref, p2f_ref, uf_ref, egf_ref, p1b_ref, p2b_ref, ub_ref, egb_ref,
                  of_ref, ob_ref, s_ref):
    c = DN_CHUNK

    @pl.when(pl.program_id(1) == 0)
    def _():
        s_ref[...] = jnp.zeros_like(s_ref)

    dirs = ((p1f_ref, p2f_ref, uf_ref, egf_ref, of_ref), (p1b_ref, p2b_ref, ub_ref, egb_ref, ob_ref))
    for d, (p1_ref, p2_ref, u_ref, eg_ref, o_ref) in enumerate(dirs):
        for h in range(HEADS):
            s = s_ref[d * HEADS + h]
            r = _dot(p1_ref[0, h], s.astype(BF16))
            vnew = u_ref[0, h].astype(F32) - r[:c]
            r2 = _dot(p2_ref[0, h], vnew.astype(BF16))
            o_ref[:, h * DV:(h + 1) * DV] = r[c:] + r2[:c]
            s_ref[d * HEADS + h] = s * eg_ref[0, h] + r2[c:]


def _dnseq(p1, p2, u, eg, batch, l, lc):
    c = DN_CHUNK
    nch = p1.shape[0]
    nt = nch * c
    nc, n = lc // c, l // c

    def fwd(b, t):
        return jnp.where(t < nc, batch * n + b * nc + t, b * n + t - nc)

    def bwd(b, t):
        return jnp.where(t < nc, batch * n + b * nc + nc - 1 - t, b * n + n - 1 - (t - nc))

    in_specs, args = [], []
    for d, cm in enumerate((fwd, bwd)):
        for arr in (p1, p2, u, eg):
            shp = (1, HEADS) + arr.shape[2:]
            in_specs.append(pl.BlockSpec(shp, functools.partial(lambda b, t, cm, d: (cm(b, t), d, 0, 0), cm=cm, d=d)))
            args.append(arr)
    return pl.pallas_call(
        _dnseq_kernel,
        grid=(batch, nc + n),
        in_specs=in_specs,
        out_specs=[
            pl.BlockSpec((c, VW), lambda b, t: (fwd(b, t), 0)),
            pl.BlockSpec((c, VW), lambda b, t: (bwd(b, t), 0)),
        ],
        out_shape=[jax.ShapeDtypeStruct((nt, VW), F32)] * 2,
        scratch_shapes=[pltpu.VMEM((2 * HEADS, DK, DV), F32)],
        compiler_params=_params(("parallel", "arbitrary")),
        name="dnseq",
    )(*args)


def _out_kernel(x_ref, m_ref, ret_ref, srg_ref, sdz_ref, sga_ref, sgb_ref, dnw_ref, wr_ref, wdn_ref, wo_ref,
                of_ref, ob_ref, out_ref):
    x = x_ref[...]
    gate = m_ref[0, 5:6, :]
    ret = (ret_ref[...].astype(F32) * srg_ref[...].astype(F32)).astype(BF16)
    yr = _dot(ret, wr_ref[...])
    dn_o = of_ref[...] + ob_ref[...]
    dnw = dnw_ref[...]
    slabs = []
    for h in range(HEADS):
        o = dn_o[:, h * DV:(h + 1) * DV]
        slabs.append(o * lax.rsqrt(jnp.mean(o * o, axis=-1, keepdims=True) + EPS) * dnw)
    dn = (jnp.concatenate(slabs, axis=1) * sdz_ref[...].astype(F32)).astype(BF16)
    yd = _dot(dn, wdn_ref[...])
    y = sga_ref[...].astype(F32) * yr + sgb_ref[...].astype(F32) * yd
    out_ref[...] = x + gate * _dot(y.astype(BF16), wo_ref[...])


def _mixer_out(xs, mods, ret, srg, sdz, sga, sgb, dn_norm_w, w_ret, w_dn, w_o, o_f, o_b, tm, tiles_per_batch, batch):
    nt = xs.shape[0]
    row = lambda w: pl.BlockSpec((tm, w), lambda i: (i, 0))
    return pl.pallas_call(
        _out_kernel,
        grid=(nt // tm,),
        in_specs=[
            row(D_MODEL), _mod_spec(tiles_per_batch, batch),
            row(VW), row(VW), row(VW), row(D_MODEL), row(D_MODEL),
            _const_spec((1, DV)),
            _const_spec((VW, D_MODEL)), _const_spec((VW, D_MODEL)), _const_spec((D_MODEL, D_MODEL)),
            row(VW), row(VW),
        ],
        out_specs=row(D_MODEL),
        out_shape=jax.ShapeDtypeStruct((nt, D_MODEL), F32),
        compiler_params=_params(("parallel",)),
        name="mixer_out",
    )(xs, mods, ret, srg, sdz, sga, sgb, dn_norm_w.reshape(1, DV).astype(F32), w_ret, w_dn, w_o, o_f, o_b)


def _rope_tables(l, tm):
    half = DK // 2
    n_freq = half // 2
    inv = ROPE_BASE ** (-jnp.arange(n_freq, dtype=F32) / n_freq)
    rows = l // GRID_W
    ang_r = jnp.arange(rows, dtype=F32)[:, None] * inv
    ang_c = jnp.arange(GRID_W, dtype=F32)[:, None] * inv
    ang = jnp.concatenate([
        jnp.broadcast_to(ang_r[:, None, :], (rows, GRID_W, n_freq)),
        jnp.broadcast_to(ang_c[None, :, :], (rows, GRID_W, n_freq)),
    ], axis=-1).reshape(l, half)
    cos, sin = jnp.cos(ang), jnp.sin(ang)
    cos_t = jnp.concatenate([jnp.concatenate([cos, cos], axis=1), jnp.ones((tm, DK), F32)], axis=0)
    sin_t = jnp.concatenate([jnp.concatenate([-sin, sin], axis=1), jnp.zeros((tm, DK), F32)], axis=0)
    return cos_t, sin_t


def kernel(x, c, ctx, c_ctx, ada_w, ada_b, norm_w, ffn1_wgu, ffn1_wd, w_in, dn_conv_w, dn_a_log, dn_dt_bias,
           dn_norm_w, w_ret_out, w_dn_out, w_o, ffn2_wgu, ffn2_wd, final_norm_w):
    batch, l, _ = x.shape
    lc = ctx.shape[1]
    depth = ada_w.shape[0]
    assert batch + 1 <= SUBLANES and l % DN_CHUNK == 0 and lc % DN_CHUNK == 0 and l % GRID_W == 0
    tm = _row_tile(l, batch * lc)
    tiles_per_batch = l // tm

    xs = jnp.concatenate([x.reshape(batch * l, D_MODEL), ctx.reshape(batch * lc, D_MODEL)], axis=0)
    cmat = jnp.zeros((SUBLANES, D_MODEL), F32).at[:batch].set(c).at[batch].set(c_ctx)
    mods_all = _ada_all(cmat, ada_w, ada_b).reshape(depth, SUBLANES, N_MOD, D_MODEL)
    cos_t, sin_t = _rope_tables(l, tm)
    log_gamma = jnp.log1p(-jnp.power(2.0, -5.0 - jnp.arange(HEADS, dtype=F32)))
    log_gamma_t = jnp.broadcast_to(log_gamma[:, None, None], (HEADS, 1, LANES))

    for layer in range(depth):
        mods = mods_all[layer, :batch + 1]
        w_l = w_in[layer]
        w_main = jnp.concatenate([w_l[:, :_OFF_GA], w_l[:, _OFF_GA + N_BA:]], axis=1).astype(BF16)
        w_ba_cols = w_l[:, _OFF_GA:_OFF_GA + N_BA]
        w_ba = jnp.pad(w_ba_cols, ((0, 0), (0, LANES - N_BA))).astype(BF16)
        w_bat = w_ba_cols.T.astype(BF16)

        xs = _ffn(xs, mods, norm_w[layer, 0], ffn1_wgu[layer].astype(BF16), ffn1_wd[layer].astype(BF16),
                  final_norm_w, 0, tm, tiles_per_batch, batch)
        rq, rk, rv, srg, dpre, sdz, ba, bat, sga, sgb = _inproj(
            xs, mods, norm_w[layer, 1], w_main, w_ba, w_bat, cos_t, sin_t, tm, tiles_per_batch, batch)
        ret = _retention(rq, rk, rv, log_gamma_t, batch, l, lc)
        p1, p2, u, eg = _dnprep(dpre, ba, bat, dn_conv_w[layer], dn_a_log[layer], dn_dt_bias[layer], batch, l, lc)
        o_f, o_b = _dnseq(p1, p2, u, eg, batch, l, lc)
        xs = _mixer_out(xs, mods, ret, srg, sdz, sga, sgb, dn_norm_w[layer], w_ret_out[layer].astype(BF16),
                        w_dn_out[layer].astype(BF16), w_o[layer].astype(BF16), o_f, o_b, tm, tiles_per_batch, batch)
        xs = _ffn(xs, mods, norm_w[layer, 2], ffn2_wgu[layer].astype(BF16), ffn2_wd[layer].astype(BF16),
                  final_norm_w, 6, tm, tiles_per_batch, batch, final=layer == depth - 1)

    return xs.reshape(batch, l, D_MODEL)
```

```python
import functools
import math

import jax
import jax.numpy as jnp
from jax import lax
from jax.experimental import pallas as pl
from jax.experimental.pallas import tpu as pltpu

F32 = jnp.float32
BF16 = jnp.bfloat16

D_MODEL = 1024
FFN_HIDDEN = 2816
N_MOD = 9
GRID_W = 64
ROPE_BASE = 10000.0
EPS = 1e-6

HEADS = 4
DK = 128
DV = 256
QK = HEADS * DK
VW = HEADS * DV
DN_CONV = 5
CONV_CH = 2 * QK + VW
N_BA = 4 * HEADS

LANES = 128
SUBLANES = 8
MXU_N = 256
VMEM_LIMIT = 56 * 1024 * 1024

DN_CHUNK = 128
DN_PREP_CHUNKS = 2
FFN_CHUNK = MXU_N


def _dot(a, b):
    return jnp.dot(a, b, preferred_element_type=F32)


def _dot_nt(a, b):
    return lax.dot_general(a, b, (((1,), (1,)), ((), ())), preferred_element_type=F32)


def _silu(x):
    return x * jax.nn.sigmoid(x)


def _row_tile(l, lc_total):
    for t in (512, 256, 128):
        if l % t == 0 and lc_total % t == 0:
            return t
    raise ValueError("sequence lengths must be multiples of 128")


def _params(sem):
    return pltpu.CompilerParams(dimension_semantics=sem, vmem_limit_bytes=VMEM_LIMIT)


def _const_spec(shape):
    nd = len(shape)
    return pl.BlockSpec(shape, lambda *_: (0,) * nd, pipeline_mode=pl.Buffered(1))


def _ada_kernel(c_ref, w_ref, b_ref, o_ref):
    c = c_ref[...]
    s = _silu(c)
    o_ref[0] = jnp.dot(s, w_ref[0], preferred_element_type=F32,
                       precision=lax.Precision.HIGHEST) + b_ref[0]


def _ada_all(cmat, ada_w, ada_b):
    depth = ada_w.shape[0]
    n = ada_w.shape[2]
    tn = 1024
    return pl.pallas_call(
        _ada_kernel,
        grid=(depth, n // tn),
        in_specs=[
            pl.BlockSpec((SUBLANES, D_MODEL), lambda l, j: (0, 0)),
            pl.BlockSpec((1, D_MODEL, tn), lambda l, j: (l, 0, j)),
            pl.BlockSpec((1, 1, tn), lambda l, j: (l, 0, j)),
        ],
        out_specs=pl.BlockSpec((1, SUBLANES, tn), lambda l, j: (l, 0, j)),
        out_shape=jax.ShapeDtypeStruct((depth, SUBLANES, n), F32),
        compiler_params=_params(("parallel", "parallel")),
        name="ada",
    )(cmat, ada_w, ada_b.reshape(depth, 1, n))


def _norm_mod(x, nw, shift, scale):
    ms = jnp.mean(x * x, axis=-1, keepdims=True)
    h = x * lax.rsqrt(ms + EPS) * nw
    return h * (1.0 + scale) + shift


def _mod_spec(tiles_per_batch, batch):
    return pl.BlockSpec((1, N_MOD, D_MODEL), lambda i: (jnp.minimum(i // tiles_per_batch, batch), 0, 0))


def _ffn_kernel(x_ref, m_ref, nw_ref, wgu_ref, wd_ref, fw_ref, o_ref, *, k0, final):
    x = x_ref[...]
    shift = m_ref[0, k0:k0 + 1, :]
    scale = m_ref[0, k0 + 1:k0 + 2, :]
    gate = m_ref[0, k0 + 2:k0 + 3, :]
    h = _norm_mod(x, nw_ref[...], shift, scale).astype(BF16)
    acc = jnp.zeros(x.shape, F32)
    for c in range(FFN_HIDDEN // FFN_CHUNK):
        lo = c * FFN_CHUNK
        g = _dot(h, wgu_ref[:, lo:lo + FFN_CHUNK])
        u = _dot(h, wgu_ref[:, FFN_HIDDEN + lo:FFN_HIDDEN + lo + FFN_CHUNK])
        a = (_silu(g) * u).astype(BF16)
        acc = acc + _dot(a, wd_ref[lo:lo + FFN_CHUNK, :])
    out = x + (0.5 * gate) * acc
    if final:
        out = out * lax.rsqrt(jnp.mean(out * out, axis=-1, keepdims=True) + EPS) * fw_ref[...]
    o_ref[...] = out


def _ffn(xs, mods, nw, wgu, wd, final_w, k0, tm, tiles_per_batch, batch, final=False):
    rows = tiles_per_batch * batch * tm if final else xs.shape[0]
    return pl.pallas_call(
        functools.partial(_ffn_kernel, k0=k0, final=final),
        grid=(rows // tm,),
        in_specs=[
            pl.BlockSpec((tm, D_MODEL), lambda i: (i, 0)),
            _mod_spec(tiles_per_batch, batch),
            _const_spec((1, D_MODEL)),
            _const_spec((D_MODEL, 2 * FFN_HIDDEN)),
            _const_spec((FFN_HIDDEN, D_MODEL)),
            _const_spec((1, D_MODEL)),
        ],
        out_specs=pl.BlockSpec((tm, D_MODEL), lambda i: (i, 0)),
        out_shape=jax.ShapeDtypeStruct((rows, D_MODEL), F32),
        compiler_params=_params(("parallel",)),
        name="ffn",
    )(xs, mods, nw.reshape(1, D_MODEL), wgu, wd, final_w.reshape(1, D_MODEL))


_OFF_RQ, _OFF_RK, _OFF_RV, _OFF_RG = 0, QK, 2 * QK, 2 * QK + VW
_OFF_DQKV = 2 * QK + 2 * VW
_OFF_DZ = _OFF_DQKV + CONV_CH
_OFF_GA = _OFF_DZ + VW
_OFF_GB = _OFF_GA + D_MODEL
_W_MAIN = _OFF_GB + D_MODEL


def _inproj_kernel(x_ref, m_ref, nw_ref, w_ref, wba_ref, wbat_ref, cos_ref, sin_ref,
                   rq_ref, rk_ref, rv_ref, srg_ref, dpre_ref, sdz_ref, ba_ref, bat_ref, sga_ref, sgb_ref):
    h = _norm_mod(x_ref[...], nw_ref[...], m_ref[0, 3:4, :], m_ref[0, 4:5, :]).astype(BF16)
    cos = cos_ref[...]
    sin = sin_ref[...]
    blk = 2 * DK

    def rope_job(o_ref, off, scale, j):
        def run():
            p = _dot(h, w_ref[:, off + j * blk:off + (j + 1) * blk]) * scale
            outs = []
            for hh in range(blk // DK):
                slab = p[:, hh * DK:(hh + 1) * DK]
                outs.append(slab * cos + pltpu.roll(slab, DK // 2, 1) * sin)
            o_ref[:, j * blk:(j + 1) * blk] = jnp.concatenate(outs, axis=1).astype(BF16)
        return run

    def act_job(o_ref, off, act, j):
        def run():
            o_ref[:, j * blk:(j + 1) * blk] = act(_dot(h, w_ref[:, off + j * blk:off + (j + 1) * blk])).astype(BF16)
        return run

    def pre_job(j):
        def run():
            dpre_ref[:, j * blk:(j + 1) * blk] = _dot(h, w_ref[:, _OFF_DQKV + j * blk:_OFF_DQKV + (j + 1) * blk])
        return run

    jobs = [rope_job(rq_ref, _OFF_RQ, 1.0, j) for j in range(QK // blk)]
    jobs += [rope_job(rk_ref, _OFF_RK, DK ** -0.5, j) for j in range(QK // blk)]
    jobs += [act_job(rv_ref, _OFF_RV, lambda p: p, j) for j in range(VW // blk)]
    jobs += [act_job(srg_ref, _OFF_RG, _silu, j) for j in range(VW // blk)]
    jobs += [act_job(sdz_ref, _OFF_DZ, _silu, j) for j in range(VW // blk)]
    jobs += [act_job(sga_ref, _OFF_GA, jax.nn.sigmoid, j) for j in range(D_MODEL // blk)]
    jobs += [act_job(sgb_ref, _OFF_GB, jax.nn.sigmoid, j) for j in range(D_MODEL // blk)]
    jobs += [pre_job(j) for j in range(CONV_CH // blk)]
    for job in jobs:
        job()
    ba_ref[...] = _dot(h, wba_ref[...])
    bat_ref[...] = _dot_nt(wbat_ref[...], h)


def _inproj(xs, mods, nw, w_main, w_ba, w_bat, cos_t, sin_t, tm, tiles_per_batch, batch):
    nt = xs.shape[0]
    n_lat_tiles = tiles_per_batch * batch

    def rope_map(i):
        return (jnp.where(i < n_lat_tiles, i % tiles_per_batch, tiles_per_batch), 0)

    row = lambda w: pl.BlockSpec((tm, w), lambda i: (i, 0))
    outs = [
        (QK, BF16), (QK, BF16), (VW, BF16), (VW, BF16), (CONV_CH, F32), (VW, BF16),
        (LANES, F32), None, (D_MODEL, BF16), (D_MODEL, BF16),
    ]
    out_specs, out_shape = [], []
    for o in outs:
        if o is None:
            out_specs.append(pl.BlockSpec((N_BA, tm), lambda i: (0, i)))
            out_shape.append(jax.ShapeDtypeStruct((N_BA, nt), F32))
        else:
            out_specs.append(row(o[0]))
            out_shape.append(jax.ShapeDtypeStruct((nt, o[0]), o[1]))
    return pl.pallas_call(
        _inproj_kernel,
        grid=(nt // tm,),
        in_specs=[
            row(D_MODEL),
            _mod_spec(tiles_per_batch, batch),
            _const_spec((1, D_MODEL)),
            _const_spec((D_MODEL, _W_MAIN)),
            _const_spec((D_MODEL, LANES)),
            _const_spec((N_BA, D_MODEL)),
            pl.BlockSpec((tm, DK), rope_map),
            pl.BlockSpec((tm, DK), rope_map),
        ],
        out_specs=out_specs,
        out_shape=out_shape,
        compiler_params=_params(("parallel",)),
        name="inproj",
    )(xs, mods, nw.reshape(1, D_MODEL), w_main, w_ba, w_bat, cos_t, sin_t)


def _ret_kernel(lg_ref, q_ref, k_ref, v_ref, o_ref,
                sf_ref, sb_ref, sball_ref, dm_ref, eq_ref, ekf_ref, ekb_ref, *, nc, n, c):
    ph = pl.program_id(1)
    t = pl.program_id(2)

    @pl.when((ph == 0) & (t == 0))
    def _():
        sb_ref[...] = jnp.zeros_like(sb_ref)
        ii = lax.broadcasted_iota(jnp.int32, (c, 1), 0).astype(F32)
        jj = lax.broadcasted_iota(jnp.int32, (1, c), 1).astype(F32)
        for h in range(HEADS):
            lg = lg_ref[h]
            dm_ref[h] = jnp.exp(lg[:, :1] * jnp.abs(ii - jj))
            eq_ref[h, :, :DK] = jnp.exp(lg * (ii + 1.0))
            eq_ref[h, :, DK:] = jnp.exp(lg * (float(c) - ii))
            ekf_ref[h] = jnp.exp(lg * (float(c) - 1.0 - ii))
            ekb_ref[h] = jnp.exp(lg * ii)

    def kv_update(h, s_ref, ek_ref):
        k = k_ref[:, h * DK:(h + 1) * DK].astype(F32) * ek_ref[h]
        kv = _dot(k.T.astype(BF16), v_ref[:, h * DV:(h + 1) * DV])
        gamma_c = jnp.exp(lg_ref[h][:, :1] * float(c))
        s_ref[h] = s_ref[h] * gamma_c + kv

    @pl.when(ph == 0)
    def _():
        pos = jnp.where(t < nc, nc - 1 - t, nc + n - 1 - (t - nc))
        for h in range(HEADS):
            sball_ref[pos, h] = sb_ref[h].astype(BF16)
            kv_update(h, sb_ref, ekb_ref)

    @pl.when(ph == 1)
    def _():
        @pl.when(t == 0)
        def _():
            sf_ref[...] = jnp.zeros_like(sf_ref)
        for h in range(HEADS):
            q = q_ref[:, h * DK:(h + 1) * DK]
            qf = q.astype(F32)
            s = _dot_nt(q, k_ref[:, h * DK:(h + 1) * DK]) * dm_ref[h]
            o = _dot(s.astype(BF16), v_ref[:, h * DV:(h + 1) * DV])
            q2 = (jnp.concatenate([qf, qf], axis=1) * eq_ref[h]).astype(BF16)
            s2 = jnp.concatenate([sf_ref[h].astype(BF16), sball_ref[t, h]], axis=0)
            o = o + _dot(q2, s2)
            o = o * lax.rsqrt(jnp.mean(o * o, axis=-1, keepdims=True) + EPS)
            o_ref[:, h * DV:(h + 1) * DV] = o.astype(BF16)
            kv_update(h, sf_ref, ekf_ref)


def _retention(rq, rk, rv, log_gamma_t, batch, l, lc):
    nt = rq.shape[0]
    c = min(256, lc)
    nc, n = lc // c, l // c
    lat0 = lambda b: b * n
    ctx0 = lambda b: batch * n + b * nc

    def chunk(b, ph, t):
        fwd = jnp.where(t < nc, ctx0(b) + t, lat0(b) + t - nc)
        bwd = jnp.where(t < nc, ctx0(b) + nc - 1 - t, lat0(b) + n - 1 - (t - nc))
        return jnp.where(ph == 0, bwd, fwd)

    def q_chunk(b, ph, t):
        return jnp.where(ph == 0, ctx0(b), chunk(b, ph, t))

    return pl.pallas_call(
        functools.partial(_ret_kernel, nc=nc, n=n, c=c),
        grid=(batch, 2, nc + n),
        in_specs=[
            _const_spec((HEADS, 1, LANES)),
            pl.BlockSpec((c, QK), lambda b, ph, t: (q_chunk(b, ph, t), 0)),
            pl.BlockSpec((c, QK), lambda b, ph, t: (chunk(b, ph, t), 0)),
            pl.BlockSpec((c, VW), lambda b, ph, t: (chunk(b, ph, t), 0)),
        ],
        out_specs=pl.BlockSpec((c, VW), lambda b, ph, t: (q_chunk(b, ph, t), 0)),
        out_shape=jax.ShapeDtypeStruct((nt, VW), BF16),
        scratch_shapes=[
            pltpu.VMEM((HEADS, DK, DV), F32),
            pltpu.VMEM((HEADS, DK, DV), F32),
            pltpu.VMEM((nc + n, HEADS, DK, DV), BF16),
            pltpu.VMEM((HEADS, c, c), F32),
            pltpu.VMEM((HEADS, c, 2 * DK), F32),
            pltpu.VMEM((HEADS, c, DK), F32),
            pltpu.VMEM((HEADS, c, DK), F32),
        ],
        compiler_params=_params(("parallel", "arbitrary", "arbitrary")),
        name="retention",
    )(log_gamma_t, rq, rk, rv)


def _mm(a, b):
    return _dot(a.astype(BF16), b.astype(BF16))


TRI_BASE = 8


def _tri_masks(ri, ci, c):
    diag = (ri // TRI_BASE) == (ci // TRI_BASE)
    offs = []
    s = TRI_BASE
    while s < c:
        offs.append(((ri // s) ^ (ci // s)) == 1)
        s *= 2
    return diag, offs


def _unit_tri_inverses(mats, eye, masks):
    diag, offs = masks
    qs = [jnp.where(diag, a, 0.0) for a in mats]
    ts = [eye - q for q in qs]
    for _ in range(int(math.log2(TRI_BASE)) - 1):
        qs = [_mm(q, q) for q in qs]
        ts = [t + _mm(t, q) for t, q in zip(ts, qs)]
    for off in offs:
        half = [_mm(t, jnp.where(off, a, 0.0)) for t, a in zip(ts, mats)]
        ts = [t - _mm(hf, t) for t, hf in zip(ts, half)]
    return ts


def _softplus(x):
    return jnp.maximum(x, 0.0) + jnp.log1p(jnp.exp(-jnp.abs(x)))


def _cumsum(x, axis, reverse):
    n = x.shape[axis]
    idx = lax.broadcasted_iota(jnp.int32, x.shape, axis)
    s = 1
    while s < n:
        if reverse:
            x = x + jnp.where(idx < n - s, pltpu.roll(x, n - s, axis), 0.0)
        else:
            x = x + jnp.where(idx >= s, pltpu.roll(x, s, axis), 0.0)
        s *= 2
    return x


def _dnprep_kernel(main_ref, prev_ref, next_ref, ba_ref, bat_ref, cw_ref, arow_ref, acol_ref, dtrow_ref, dtcol_ref,
                   p1_ref, p2_ref, u_ref, eg_ref, ext_ref, nat_ref, *, nb, lat_blocks, ctx_blocks, n_lat):
    c = DN_CHUNK
    rows = nb * c
    g = pl.program_id(0)
    in_seq = jnp.where(g < n_lat, g % lat_blocks, (g - n_lat) % ctx_blocks)
    seq_len = jnp.where(g < n_lat, lat_blocks, ctx_blocks)

    n_slab = CONV_CH // LANES
    prev = jnp.where(in_seq == 0, 0.0, prev_ref[...])
    nxt = jnp.where(in_seq == seq_len - 1, 0.0, next_ref[...])
    for sl in range(n_slab):
        cols = slice(sl * LANES, (sl + 1) * LANES)
        ext_ref[sl, 0:SUBLANES, :] = prev[:, cols]
        ext_ref[sl, SUBLANES:SUBLANES + rows, :] = main_ref[:, cols]
        ext_ref[sl, SUBLANES + rows:2 * SUBLANES + rows, :] = nxt[:, cols]
    phases = c // SUBLANES
    pad = DN_CONV // 2
    zcache = {}

    def phase_rows(n, t):
        if (n, t) not in zcache:
            zcache[n, t] = jnp.concatenate(
                [ext_ref[sl, pl.ds(SUBLANES + n * c + t, SUBLANES, stride=phases), :] for sl in range(n_slab)], axis=1)
        return zcache[n, t]

    def conv_phase(n, j):
        acc = phase_rows(n, j) * cw_ref[pad:pad + 1, :]
        for k in range(DN_CONV):
            if k != pad:
                acc = acc + phase_rows(n, j + k - pad) * cw_ref[k:k + 1, :]
        yj = _silu(acc)
        parts = []
        for hh in range(2 * HEADS):
            s = yj[:, hh * DK:(hh + 1) * DK]
            scale = (DK ** -0.5) if hh < HEADS else 1.0
            parts.append(s * (lax.rsqrt(jnp.sum(s * s, axis=-1, keepdims=True) + EPS) * scale))
        for sl in range(2 * HEADS, n_slab):
            parts.append(yj[:, sl * LANES:(sl + 1) * LANES])
        for sl in range(n_slab):
            nat_ref[sl, pl.ds(n * c + j, SUBLANES, stride=phases), :] = parts[sl]

    ri = lax.broadcasted_iota(jnp.int32, (c, c), 0)
    ci = lax.broadcasted_iota(jnp.int32, (c, c), 1)
    eye = (ri == ci).astype(F32)
    masks = _tri_masks(ri, ci, c)
    qs, ks, vs, kts, qks = {}, {}, {}, {}, {}

    def build_items(n):
        items = []
        tok = slice(n * c, (n + 1) * c)
        ba = ba_ref[tok, :]
        sig = jax.nn.sigmoid(ba)
        g_col = -jnp.exp(arow_ref[...]) * _softplus(ba + dtrow_ref[...])
        gc_col = (_cumsum(g_col, 0, False), _cumsum(g_col, 0, True))
        g_row = -jnp.exp(acol_ref[...]) * _softplus(bat_ref[:, tok] + dtcol_ref[...])
        gc_row = (_cumsum(g_row, 1, False), _cumsum(g_row, 1, True))
        for h in range(HEADS):
            q = nat_ref[h, tok, :]
            k = nat_ref[HEADS + h, tok, :]
            kb16 = k.astype(BF16)
            qs[n, h] = q
            ks[n, h] = k
            vs[n, h] = jnp.concatenate([nat_ref[2 * HEADS + 2 * h, tok, :], nat_ref[2 * HEADS + 2 * h + 1, tok, :]],
                                       axis=1)
            kts[n, h] = k.T
            qks[n, h] = _dot_nt(q.astype(BF16), kb16)
            kk = _dot_nt(kb16, kb16)
            for d in range(2):
                lane = 2 * HEADS + d * HEADS + h
                gcol = gc_col[d][:, lane:lane + 1]
                grow = gc_row[d][lane:lane + 1, :]
                incl = (ri >= ci) if d == 0 else (ri <= ci)
                strict = (ri > ci) if d == 0 else (ri < ci)
                decay = jnp.where(incl, jnp.exp(jnp.where(incl, gcol - grow, 0.0)), 0.0)
                beta = sig[:, d * HEADS + h:d * HEADS + h + 1]
                a = jnp.where(strict, beta * kk * decay, 0.0)
                items.append((n, h, d, gcol, grow, decay, beta, a))
        return items

    def emit(item, tmat):
        n, h, d, gcol, grow, decay, beta, _ = item
        hd = d * HEADS + h
        q, k, v = qs[n, h], ks[n, h], vs[n, h]
        egc = jnp.exp(gcol)
        rhs = jnp.concatenate([v * beta, k * (beta * egc)], axis=1).astype(BF16)
        uw = _dot(tmat.astype(BF16), rhs)
        glast = grow[:, c - 1:c] if d == 0 else grow[:, 0:1]
        u_ref[n, hd] = uw[:, :DV].astype(BF16)
        p1_ref[n, hd, 0:c, :] = uw[:, DV:].astype(BF16)
        p1_ref[n, hd, c:2 * c, :] = (q * egc).astype(BF16)
        p2_ref[n, hd, 0:c, :] = (qks[n, h] * decay).astype(BF16)
        p2_ref[n, hd, c:2 * c, :] = (kts[n, h] * jnp.exp(glast - grow)).astype(BF16)
        eg_ref[n, hd] = jnp.broadcast_to(jnp.exp(glast), (1, DV))

    for n in range(nb):
        for j in range(phases):
            conv_phase(n, j)
    items = [it for n in range(nb) for it in build_items(n)]
    for it, tmat in zip(items, _unit_tri_inverses([it[7] for it in items], eye, masks)):
        emit(it, tmat)


def _dnprep(dpre, ba, bat, conv_w, a_log, dt_bias, batch, l, lc):
    nt = dpre.shape[0]
    c = DN_CHUNK
    nch = nt // c
    nblk8 = nt // SUBLANES
    flat_a = jnp.concatenate([jnp.zeros((2 * HEADS,), F32), a_log.reshape(-1).astype(F32)])
    flat_dt = jnp.concatenate([jnp.zeros((2 * HEADS,), F32), dt_bias.reshape(-1).astype(F32)])
    arow = jnp.pad(flat_a, (0, LANES - N_BA)).reshape(1, LANES)
    dtrow = jnp.pad(flat_dt, (0, LANES - N_BA)).reshape(1, LANES)
    acol = flat_a.reshape(N_BA, 1)
    dtcol = flat_dt.reshape(N_BA, 1)
    hd = 2 * HEADS
    nb = DN_PREP_CHUNKS if (l // c) % DN_PREP_CHUNKS == 0 and (lc // c) % DN_PREP_CHUNKS == 0 else 1
    rows = nb * c
    per = rows // SUBLANES
    return pl.pallas_call(
        functools.partial(_dnprep_kernel, nb=nb, lat_blocks=l // rows, ctx_blocks=lc // rows,
                          n_lat=batch * l // rows),
        grid=(nch // nb,),
        in_specs=[
            pl.BlockSpec((rows, CONV_CH), lambda g: (g, 0)),
            pl.BlockSpec((SUBLANES, CONV_CH), lambda g: (jnp.maximum(g * per - 1, 0), 0)),
            pl.BlockSpec((SUBLANES, CONV_CH), lambda g: (jnp.minimum((g + 1) * per, nblk8 - 1), 0)),
            pl.BlockSpec((rows, LANES), lambda g: (g, 0)),
            pl.BlockSpec((N_BA, rows), lambda g: (0, g)),
            _const_spec((DN_CONV, CONV_CH)),
            _const_spec((1, LANES)),
            _const_spec((N_BA, 1)),
            _const_spec((1, LANES)),
            _const_spec((N_BA, 1)),
        ],
        out_specs=[
            pl.BlockSpec((nb, hd, 2 * c, DK), lambda g: (g, 0, 0, 0)),
            pl.BlockSpec((nb, hd, 2 * c, c), lambda g: (g, 0, 0, 0)),
            pl.BlockSpec((nb, hd, c, DV), lambda g: (g, 0, 0, 0)),
            pl.BlockSpec((nb, hd, 1, DV), lambda g: (g, 0, 0, 0)),
        ],
        out_shape=[
            jax.ShapeDtypeStruct((nch, hd, 2 * c, DK), BF16),
            jax.ShapeDtypeStruct((nch, hd, 2 * c, c), BF16),
            jax.ShapeDtypeStruct((nch, hd, c, DV), BF16),
            jax.ShapeDtypeStruct((nch, hd, 1, DV), F32),
        ],
        scratch_shapes=[pltpu.VMEM((CONV_CH // LANES, rows + 2 * SUBLANES, LANES), F32),
                        pltpu.VMEM((CONV_CH // LANES, rows, LANES), F32)],
        compiler_params=_params(("parallel",)),
        name="dnprep",
    )(dpre, dpre, dpre, ba, bat, conv_w.astype(F32), arow, acol, dtrow, dtcol)


def _dnseq_kernel(*refs, batch):
    c = DN_CHUNK
    ins = refs[:8 * batch]
    of_ref, ob_ref, s_ref = refs[8 * batch:]

    @pl.when(pl.program_id(0) == 0)
    def _():
        s_ref[...] = jnp.zeros_like(s_ref)

    chains = []
    for b in range(batch):
        for d in range(2):
            p1_ref, p2_ref, u_ref, eg_ref = ins[8 * b + 4 * d:8 * b + 4 * d + 4]
            for h in range(HEADS):
                chains.append((b, d, h, p1_ref, p2_ref, u_ref, eg_ref))
    sidx = lambda b, d, h: (b * 2 + d) * HEADS + h
    rs = [_dot(p1_ref[0, h], s_ref[sidx(b, d, h)].astype(BF16))
          for (b, d, h, p1_ref, _, _, _) in chains]
    vnews = [(u_ref[0, h].astype(F32) - r[:c]).astype(BF16)
             for (b, d, h, _, _, u_ref, _), r in zip(chains, rs)]
    r2s = [_dot(p2_ref[0, h], vnew)
           for (b, d, h, _, p2_ref, _, _), vnew in zip(chains, vnews)]
    for (b, d, h, _, _, _, eg_ref), r, r2 in zip(chains, rs, r2s):
        o_ref = of_ref if d == 0 else ob_ref
        lo = b * VW + h * DV
        o_ref[:, lo:lo + DV] = r[c:] + r2[:c]
        s_ref[sidx(b, d, h)] = s_ref[sidx(b, d, h)] * eg_ref[0, h] + r2[c:]


def _dnseq(p1, p2, u, eg, batch, l, lc):
    c = DN_CHUNK
    nc, n = lc // c, l // c

    def fwd(b, t):
        return jnp.where(t < nc, batch * n + b * nc + t, b * n + t - nc)

    def bwd(b, t):
        return jnp.where(t < nc, batch * n + b * nc + nc - 1 - t, b * n + n - 1 - (t - nc))

    in_specs, args = [], []
    for b in range(batch):
        for d, cm in enumerate((fwd, bwd)):
            for arr in (p1, p2, u, eg):
                shp = (1, HEADS) + arr.shape[2:]
                in_specs.append(pl.BlockSpec(
                    shp, functools.partial(lambda t, cm, b, d: (cm(b, t), d, 0, 0), cm=cm, b=b, d=d)))
                args.append(arr)
    return pl.pallas_call(
        functools.partial(_dnseq_kernel, batch=batch),
        grid=(nc + n,),
        in_specs=in_specs,
        out_specs=[
            pl.BlockSpec((c, batch * VW), lambda t: (jnp.where(t < nc, n + t, t - nc), 0)),
            pl.BlockSpec((c, batch * VW), lambda t: (jnp.where(t < nc, n + nc - 1 - t, n - 1 - (t - nc)), 0)),
        ],
        out_shape=[jax.ShapeDtypeStruct((l + lc, batch * VW), F32)] * 2,
        scratch_shapes=[pltpu.VMEM((batch * 2 * HEADS, DK, DV), F32)],
        compiler_params=_params(("arbitrary",)),
        name="dnseq",
    )(*args)


def _out_kernel(*refs, batch, n_lat_tiles):
    (x_ref, m_ref, ret_ref, srg_ref, sdz_ref, sga_ref, sgb_ref, dnw_ref, wr_ref, wdn_ref, wo_ref,
     of_ref, ob_ref) = refs[:13]
    ctx_refs = refs[13:13 + 2 * batch]
    out_ref = refs[13 + 2 * batch]
    x = x_ref[...]
    gate = m_ref[0, 5:6, :]
    ret = (ret_ref[...].astype(F32) * srg_ref[...].astype(F32)).astype(BF16)
    yr = _dot(ret, wr_ref[...])
    dn_lat = of_ref[...] + ob_ref[...]
    dn_ctx = jnp.concatenate([ctx_refs[2 * b][...] + ctx_refs[2 * b + 1][...] for b in range(batch)], axis=0)
    dn_o = jnp.where(pl.program_id(0) < n_lat_tiles, dn_lat, dn_ctx)
    dnw = dnw_ref[...]
    slabs = []
    for h in range(HEADS):
        o = dn_o[:, h * DV:(h + 1) * DV]
        slabs.append(o * lax.rsqrt(jnp.mean(o * o, axis=-1, keepdims=True) + EPS) * dnw)
    dn = (jnp.concatenate(slabs, axis=1) * sdz_ref[...].astype(F32)).astype(BF16)
    yd = _dot(dn, wdn_ref[...])
    y = sga_ref[...].astype(F32) * yr + sgb_ref[...].astype(F32) * yd
    out_ref[...] = x + gate * _dot(y.astype(BF16), wo_ref[...])


def _mixer_out(xs, mods, ret, srg, sdz, sga, sgb, dn_norm_w, w_ret, w_dn, w_o, o_f, o_b,
               tm, tiles_per_batch, batch, l, lc):
    nt = xs.shape[0]
    n_lat_tiles = tiles_per_batch * batch
    assert nt // tm == n_lat_tiles + 1 and batch * lc == tm
    row = lambda w: pl.BlockSpec((tm, w), lambda i: (i, 0))
    lat = pl.BlockSpec((tm, VW), lambda i: (jnp.where(i < n_lat_tiles, i % tiles_per_batch, 0),
                                            jnp.where(i < n_lat_tiles, i // tiles_per_batch, 0)))
    ctx_specs, ctx_args = [], []
    for b in range(batch):
        for arr in (o_f, o_b):
            ctx_specs.append(pl.BlockSpec((lc, VW), functools.partial(lambda i, b: (l // lc, b), b=b)))
            ctx_args.append(arr)
    return pl.pallas_call(
        functools.partial(_out_kernel, batch=batch, n_lat_tiles=n_lat_tiles),
        grid=(nt // tm,),
        in_specs=[
            row(D_MODEL), _mod_spec(tiles_per_batch, batch),
            row(VW), row(VW), row(VW), row(D_MODEL), row(D_MODEL),
            _const_spec((1, DV)),
            _const_spec((VW, D_MODEL)), _const_spec((VW, D_MODEL)), _const_spec((D_MODEL, D_MODEL)),
            lat, lat,
        ] + ctx_specs,
        out_specs=row(D_MODEL),
        out_shape=jax.ShapeDtypeStruct((nt, D_MODEL), F32),
        compiler_params=_params(("parallel",)),
        name="mixer_out",
    )(xs, mods, ret, srg, sdz, sga, sgb, dn_norm_w.reshape(1, DV).astype(F32), w_ret, w_dn, w_o, o_f, o_b,
      *ctx_args)


def _rope_tables(l, tm):
    half = DK // 2
    n_freq = half // 2
    inv = ROPE_BASE ** (-jnp.arange(n_freq, dtype=F32) / n_freq)
    rows = l // GRID_W
    ang_r = jnp.arange(rows, dtype=F32)[:, None] * inv
    ang_c = jnp.arange(GRID_W, dtype=F32)[:, None] * inv
    ang = jnp.concatenate([
        jnp.broadcast_to(ang_r[:, None, :], (rows, GRID_W, n_freq)),
        jnp.broadcast_to(ang_c[None, :, :], (rows, GRID_W, n_freq)),
    ], axis=-1).reshape(l, half)
    cos, sin = jnp.cos(ang), jnp.sin(ang)
    cos_t = jnp.concatenate([jnp.concatenate([cos, cos], axis=1), jnp.ones((tm, DK), F32)], axis=0)
    sin_t = jnp.concatenate([jnp.concatenate([-sin, sin], axis=1), jnp.zeros((tm, DK), F32)], axis=0)
    return cos_t, sin_t


def kernel(x, c, ctx, c_ctx, ada_w, ada_b, norm_w, ffn1_wgu, ffn1_wd, w_in, dn_conv_w, dn_a_log, dn_dt_bias,
           dn_norm_w, w_ret_out, w_dn_out, w_o, ffn2_wgu, ffn2_wd, final_norm_w):
    batch, l, _ = x.shape
    lc = ctx.shape[1]
    depth = ada_w.shape[0]
    assert batch + 1 <= SUBLANES and l % DN_CHUNK == 0 and lc % DN_CHUNK == 0 and l % GRID_W == 0
    tm = _row_tile(l, batch * lc)
    tiles_per_batch = l // tm

    xs = jnp.concatenate([x.reshape(batch * l, D_MODEL), ctx.reshape(batch * lc, D_MODEL)], axis=0)
    cmat = jnp.zeros((SUBLANES, D_MODEL), F32).at[:batch].set(c).at[batch].set(c_ctx)
    mods_all = _ada_all(cmat, ada_w, ada_b).reshape(depth, SUBLANES, N_MOD, D_MODEL)
    cos_t, sin_t = _rope_tables(l, tm)
    log_gamma = jnp.log1p(-jnp.power(2.0, -5.0 - jnp.arange(HEADS, dtype=F32)))
    log_gamma_t = jnp.broadcast_to(log_gamma[:, None, None], (HEADS, 1, LANES))

    for layer in range(depth):
        mods = mods_all[layer, :batch + 1]
        w_l = w_in[layer]
        w_main = jnp.concatenate([w_l[:, :_OFF_GA], w_l[:, _OFF_GA + N_BA:]], axis=1).astype(BF16)
        w_ba_cols = w_l[:, _OFF_GA:_OFF_GA + N_BA]
        w_ba = jnp.pad(w_ba_cols, ((0, 0), (0, LANES - N_BA))).astype(BF16)
        w_bat = w_ba_cols.T.astype(BF16)

        xs = _ffn(xs, mods, norm_w[layer, 0], ffn1_wgu[layer].astype(BF16), ffn1_wd[layer].astype(BF16),
                  final_norm_w, 0, tm, tiles_per_batch, batch)
        rq, rk, rv, srg, dpre, sdz, ba, bat, sga, sgb = _inproj(
            xs, mods, norm_w[layer, 1], w_main, w_ba, w_bat, cos_t, sin_t, tm, tiles_per_batch, batch)
        ret = _retention(rq, rk, rv, log_gamma_t, batch, l, lc)
        p1, p2, u, eg = _dnprep(dpre, ba, bat, dn_conv_w[layer], dn_a_log[layer], dn_dt_bias[layer], batch, l, lc)
        o_f, o_b = _dnseq(p1, p2, u, eg, batch, l, lc)
        xs = _mixer_out(xs, mods, ret, srg, sdz, sga, sgb, dn_norm_w[layer], w_ret_out[layer].astype(BF16),
                        w_dn_out[layer].astype(BF16), w_o[layer].astype(BF16), o_f, o_b,
                        tm, tiles_per_batch, batch, l, lc)
        xs = _ffn(xs, mods, norm_w[layer, 2], ffn2_wgu[layer].astype(BF16), ffn2_wd[layer].astype(BF16),
                  final_norm_w, 6, tm, tiles_per_batch, batch, final=layer == depth - 1)

    return xs.reshape(batch, l, D_MODEL)
```

```python
import functools
import math

import jax
import jax.numpy as jnp
from jax import lax
from jax.experimental import pallas as pl
from jax.experimental.pallas import tpu as pltpu

F32 = jnp.float32
BF16 = jnp.bfloat16

D_MODEL = 1024
FFN_HIDDEN = 2816
N_MOD = 9
GRID_W = 64
ROPE_BASE = 10000.0
EPS = 1e-6

HEADS = 4
DK = 128
DV = 256
QK = HEADS * DK
VW = HEADS * DV
DN_CONV = 5
CONV_CH = 2 * QK + VW
N_BA = 4 * HEADS

LANES = 128
SUBLANES = 8
MXU_N = 256
VMEM_LIMIT = 56 * 1024 * 1024

DN_CHUNK = 128
DN_PREP_CHUNKS = 2
FFN_CHUNK = MXU_N


def _dot(a, b):
    return jnp.dot(a, b, preferred_element_type=F32)


def _dot_nt(a, b):
    return lax.dot_general(a, b, (((1,), (1,)), ((), ())), preferred_element_type=F32)


def _silu(x):
    return x * jax.nn.sigmoid(x)


def _row_tile(l, lc_total):
    for t in (512, 256, 128):
        if l % t == 0 and lc_total % t == 0:
            return t
    raise ValueError("sequence lengths must be multiples of 128")


def _params(sem):
    return pltpu.CompilerParams(dimension_semantics=sem, vmem_limit_bytes=VMEM_LIMIT)


def _const_spec(shape):
    nd = len(shape)
    return pl.BlockSpec(shape, lambda *_: (0,) * nd, pipeline_mode=pl.Buffered(1))


def _ada_kernel(c_ref, w_ref, b_ref, o_ref):
    c = c_ref[...]
    s = _silu(c)
    o_ref[0] = jnp.dot(s, w_ref[0], preferred_element_type=F32,
                       precision=lax.Precision.HIGHEST) + b_ref[0]


def _ada_all(cmat, ada_w, ada_b):
    depth = ada_w.shape[0]
    n = ada_w.shape[2]
    tn = 1024
    return pl.pallas_call(
        _ada_kernel,
        grid=(depth, n // tn),
        in_specs=[
            pl.BlockSpec((SUBLANES, D_MODEL), lambda l, j: (0, 0)),
            pl.BlockSpec((1, D_MODEL, tn), lambda l, j: (l, 0, j)),
            pl.BlockSpec((1, 1, tn), lambda l, j: (l, 0, j)),
        ],
        out_specs=pl.BlockSpec((1, SUBLANES, tn), lambda l, j: (l, 0, j)),
        out_shape=jax.ShapeDtypeStruct((depth, SUBLANES, n), F32),
        compiler_params=_params(("parallel", "parallel")),
        name="ada",
    )(cmat, ada_w, ada_b.reshape(depth, 1, n))


def _norm_mod(x, nw, shift, scale):
    ms = jnp.mean(x * x, axis=-1, keepdims=True)
    h = x * lax.rsqrt(ms + EPS) * nw
    return h * (1.0 + scale) + shift


def _mod_spec(tiles_per_batch, batch):
    return pl.BlockSpec((1, N_MOD, D_MODEL), lambda i: (jnp.minimum(i // tiles_per_batch, batch), 0, 0))


def _ffn_kernel(x_ref, c_ref, m_ref, nw_ref, wgu_ref, wd_ref, fw_ref, o_ref, *, k0, final, n_lat_tiles):
    x = x_ref[...] if n_lat_tiles is None else jnp.where(pl.program_id(0) < n_lat_tiles, x_ref[...], c_ref[...])
    shift = m_ref[0, k0:k0 + 1, :]
    scale = m_ref[0, k0 + 1:k0 + 2, :]
    gate = m_ref[0, k0 + 2:k0 + 3, :]
    h = _norm_mod(x, nw_ref[...], shift, scale).astype(BF16)
    acc = jnp.zeros(x.shape, F32)
    for c in range(FFN_HIDDEN // FFN_CHUNK):
        lo = c * FFN_CHUNK
        g = _dot(h, wgu_ref[:, lo:lo + FFN_CHUNK])
        u = _dot(h, wgu_ref[:, FFN_HIDDEN + lo:FFN_HIDDEN + lo + FFN_CHUNK])
        a = (_silu(g) * u).astype(BF16)
        acc = acc + _dot(a, wd_ref[lo:lo + FFN_CHUNK, :])
    out = x + (0.5 * gate) * acc
    if final:
        out = out * lax.rsqrt(jnp.mean(out * out, axis=-1, keepdims=True) + EPS) * fw_ref[...]
    o_ref[...] = out


def _ffn(xs, mods, nw, wgu, wd, final_w, k0, tm, tiles_per_batch, batch, final=False, ctx_rows=None):
    n_lat_tiles = tiles_per_batch * batch
    if ctx_rows is None:
        rows = n_lat_tiles * tm if final else xs.shape[0]
        x_spec = pl.BlockSpec((tm, D_MODEL), lambda i: (i, 0))
        c_arg, c_spec = final_w.reshape(1, D_MODEL), _const_spec((1, D_MODEL))
    else:
        assert ctx_rows.shape[0] == tm and not final
        rows = (n_lat_tiles + 1) * tm
        x_spec = pl.BlockSpec((tm, D_MODEL), lambda i: (jnp.minimum(i, n_lat_tiles - 1), 0))
        c_arg, c_spec = ctx_rows, _const_spec((tm, D_MODEL))
    return pl.pallas_call(
        functools.partial(_ffn_kernel, k0=k0, final=final, n_lat_tiles=None if ctx_rows is None else n_lat_tiles),
        grid=(rows // tm,),
        in_specs=[
            x_spec,
            c_spec,
            _mod_spec(tiles_per_batch, batch),
            _const_spec((1, D_MODEL)),
            _const_spec((D_MODEL, 2 * FFN_HIDDEN)),
            _const_spec((FFN_HIDDEN, D_MODEL)),
            _const_spec((1, D_MODEL)),
        ],
        out_specs=pl.BlockSpec((tm, D_MODEL), lambda i: (i, 0)),
        out_shape=jax.ShapeDtypeStruct((rows, D_MODEL), F32),
        compiler_params=_params(("parallel",)),
        name="ffn",
    )(xs, c_arg, mods, nw.reshape(1, D_MODEL), wgu, wd, final_w.reshape(1, D_MODEL))


_OFF_RQ, _OFF_RK, _OFF_RV, _OFF_RG = 0, QK, 2 * QK, 2 * QK + VW
_OFF_DQKV = 2 * QK + 2 * VW
_OFF_DZ = _OFF_DQKV + CONV_CH
_OFF_GA = _OFF_DZ + VW
_OFF_GB = _OFF_GA + D_MODEL
_W_MAIN = _OFF_GB + D_MODEL


def _inproj_kernel(x_ref, m_ref, nw_ref, w_ref, wba_ref, wbat_ref, cos_ref, sin_ref,
                   rq_ref, rk_ref, rv_ref, srg_ref, dpre_ref, sdz_ref, ba_ref, bat_ref, sga_ref, sgb_ref):
    h = _norm_mod(x_ref[...], nw_ref[...], m_ref[0, 3:4, :], m_ref[0, 4:5, :]).astype(BF16)
    cos = cos_ref[...]
    sin = sin_ref[...]
    blk = 2 * DK

    def rope_job(o_ref, off, scale, j):
        def run():
            p = _dot(h, w_ref[:, off + j * blk:off + (j + 1) * blk]) * scale
            outs = []
            for hh in range(blk // DK):
                slab = p[:, hh * DK:(hh + 1) * DK]
                outs.append(slab * cos + pltpu.roll(slab, DK // 2, 1) * sin)
            o_ref[:, j * blk:(j + 1) * blk] = jnp.concatenate(outs, axis=1).astype(BF16)
        return run

    def act_job(o_ref, off, act, j):
        def run():
            o_ref[:, j * blk:(j + 1) * blk] = act(_dot(h, w_ref[:, off + j * blk:off + (j + 1) * blk])).astype(BF16)
        return run

    def pre_job(j):
        def run():
            dpre_ref[:, j * blk:(j + 1) * blk] = _dot(h, w_ref[:, _OFF_DQKV + j * blk:_OFF_DQKV + (j + 1) * blk])
        return run

    jobs = [rope_job(rq_ref, _OFF_RQ, 1.0, j) for j in range(QK // blk)]
    jobs += [rope_job(rk_ref, _OFF_RK, DK ** -0.5, j) for j in range(QK // blk)]
    jobs += [act_job(rv_ref, _OFF_RV, lambda p: p, j) for j in range(VW // blk)]
    jobs += [act_job(srg_ref, _OFF_RG, _silu, j) for j in range(VW // blk)]
    jobs += [act_job(sdz_ref, _OFF_DZ, _silu, j) for j in range(VW // blk)]
    jobs += [act_job(sga_ref, _OFF_GA, jax.nn.sigmoid, j) for j in range(D_MODEL // blk)]
    jobs += [act_job(sgb_ref, _OFF_GB, jax.nn.sigmoid, j) for j in range(D_MODEL // blk)]
    jobs += [pre_job(j) for j in range(CONV_CH // blk)]
    for job in jobs:
        job()
    ba_ref[...] = _dot(h, wba_ref[...])
    bat_ref[...] = _dot_nt(wbat_ref[...], h)


def _inproj(xs, mods, nw, w_main, w_ba, w_bat, cos_t, sin_t, tm, tiles_per_batch, batch):
    nt = xs.shape[0]
    n_lat_tiles = tiles_per_batch * batch

    def rope_map(i):
        return (jnp.where(i < n_lat_tiles, i % tiles_per_batch, tiles_per_batch), 0)

    row = lambda w: pl.BlockSpec((tm, w), lambda i: (i, 0))
    outs = [
        (QK, BF16), (QK, BF16), (VW, BF16), (VW, BF16), (CONV_CH, F32), (VW, BF16),
        (LANES, F32), None, (D_MODEL, BF16), (D_MODEL, BF16),
    ]
    out_specs, out_shape = [], []
    for o in outs:
        if o is None:
            out_specs.append(pl.BlockSpec((N_BA, tm), lambda i: (0, i)))
            out_shape.append(jax.ShapeDtypeStruct((N_BA, nt), F32))
        else:
            out_specs.append(row(o[0]))
            out_shape.append(jax.ShapeDtypeStruct((nt, o[0]), o[1]))
    return pl.pallas_call(
        _inproj_kernel,
        grid=(nt // tm,),
        in_specs=[
            row(D_MODEL),
            _mod_spec(tiles_per_batch, batch),
            _const_spec((1, D_MODEL)),
            _const_spec((D_MODEL, _W_MAIN)),
            _const_spec((D_MODEL, LANES)),
            _const_spec((N_BA, D_MODEL)),
            pl.BlockSpec((tm, DK), rope_map),
            pl.BlockSpec((tm, DK), rope_map),
        ],
        out_specs=out_specs,
        out_shape=out_shape,
        compiler_params=_params(("parallel",)),
        name="inproj",
    )(xs, mods, nw.reshape(1, D_MODEL), w_main, w_ba, w_bat, cos_t, sin_t)


def _ret_kernel(lg_ref, q_ref, k_ref, v_ref, o_ref,
                sf_ref, sb_ref, sball_ref, dm_ref, eq_ref, ekf_ref, ekb_ref, *, nc, n, c):
    ph = pl.program_id(1)
    t = pl.program_id(2)

    @pl.when((ph == 0) & (t == 0))
    def _():
        sb_ref[...] = jnp.zeros_like(sb_ref)
        ii = lax.broadcasted_iota(jnp.int32, (c, 1), 0).astype(F32)
        jj = lax.broadcasted_iota(jnp.int32, (1, c), 1).astype(F32)
        for h in range(HEADS):
            lg = lg_ref[h]
            dm_ref[h] = jnp.exp(lg[:, :1] * jnp.abs(ii - jj))
            eq_ref[h, :, :DK] = jnp.exp(lg * (ii + 1.0))
            eq_ref[h, :, DK:] = jnp.exp(lg * (float(c) - ii))
            ekf_ref[h] = jnp.exp(lg * (float(c) - 1.0 - ii))
            ekb_ref[h] = jnp.exp(lg * ii)

    def kv_update(h, s_ref, ek_ref):
        k = k_ref[:, h * DK:(h + 1) * DK].astype(F32) * ek_ref[h]
        kv = _dot(k.T.astype(BF16), v_ref[:, h * DV:(h + 1) * DV])
        gamma_c = jnp.exp(lg_ref[h][:, :1] * float(c))
        s_ref[h] = s_ref[h] * gamma_c + kv

    @pl.when(ph == 0)
    def _():
        pos = jnp.where(t < nc, nc - 1 - t, nc + n - 1 - (t - nc))
        for h in range(HEADS):
            sball_ref[pos, h] = sb_ref[h].astype(BF16)
            kv_update(h, sb_ref, ekb_ref)

    @pl.when(ph == 1)
    def _():
        @pl.when(t == 0)
        def _():
            sf_ref[...] = jnp.zeros_like(sf_ref)
        for h in range(HEADS):
            q = q_ref[:, h * DK:(h + 1) * DK]
            qf = q.astype(F32)
            s = _dot_nt(q, k_ref[:, h * DK:(h + 1) * DK]) * dm_ref[h]
            o = _dot(s.astype(BF16), v_ref[:, h * DV:(h + 1) * DV])
            q2 = (jnp.concatenate([qf, qf], axis=1) * eq_ref[h]).astype(BF16)
            s2 = jnp.concatenate([sf_ref[h].astype(BF16), sball_ref[t, h]], axis=0)
            o = o + _dot(q2, s2)
            o = o * lax.rsqrt(jnp.mean(o * o, axis=-1, keepdims=True) + EPS)
            o_ref[:, h * DV:(h + 1) * DV] = o.astype(BF16)
            kv_update(h, sf_ref, ekf_ref)


def _retention(rq, rk, rv, log_gamma_t, batch, l, lc):
    nt = rq.shape[0]
    c = min(256, lc)
    nc, n = lc // c, l // c
    lat0 = lambda b: b * n
    ctx0 = lambda b: batch * n + b * nc

    def chunk(b, ph, t):
        fwd = jnp.where(t < nc, ctx0(b) + t, lat0(b) + t - nc)
        bwd = jnp.where(t < nc, ctx0(b) + nc - 1 - t, lat0(b) + n - 1 - (t - nc))
        return jnp.where(ph == 0, bwd, fwd)

    def q_chunk(b, ph, t):
        return jnp.where(ph == 0, ctx0(b), chunk(b, ph, t))

    return pl.pallas_call(
        functools.partial(_ret_kernel, nc=nc, n=n, c=c),
        grid=(batch, 2, nc + n),
        in_specs=[
            _const_spec((HEADS, 1, LANES)),
            pl.BlockSpec((c, QK), lambda b, ph, t: (q_chunk(b, ph, t), 0)),
            pl.BlockSpec((c, QK), lambda b, ph, t: (chunk(b, ph, t), 0)),
            pl.BlockSpec((c, VW), lambda b, ph, t: (chunk(b, ph, t), 0)),
        ],
        out_specs=pl.BlockSpec((c, VW), lambda b, ph, t: (q_chunk(b, ph, t), 0)),
        out_shape=jax.ShapeDtypeStruct((nt, VW), BF16),
        scratch_shapes=[
            pltpu.VMEM((HEADS, DK, DV), F32),
            pltpu.VMEM((HEADS, DK, DV), F32),
            pltpu.VMEM((nc + n, HEADS, DK, DV), BF16),
            pltpu.VMEM((HEADS, c, c), F32),
            pltpu.VMEM((HEADS, c, 2 * DK), F32),
            pltpu.VMEM((HEADS, c, DK), F32),
            pltpu.VMEM((HEADS, c, DK), F32),
        ],
        compiler_params=_params(("parallel", "arbitrary", "arbitrary")),
        name="retention",
    )(log_gamma_t, rq, rk, rv)


def _mm(a, b):
    return _dot(a.astype(BF16), b.astype(BF16))


TRI_BASE = 8


def _tri_masks(ri, ci, c):
    diag = (ri // TRI_BASE) == (ci // TRI_BASE)
    offs = []
    s = TRI_BASE
    while s < c:
        offs.append(((ri // s) ^ (ci // s)) == 1)
        s *= 2
    return diag, offs


def _unit_tri_inverses(mats, eye, masks):
    diag, offs = masks
    qs = [jnp.where(diag, a, 0.0) for a in mats]
    ts = [eye - q for q in qs]
    for _ in range(int(math.log2(TRI_BASE)) - 1):
        qs = [_mm(q, q) for q in qs]
        ts = [t + _mm(t, q) for t, q in zip(ts, qs)]
    for off in offs:
        half = [_mm(t, jnp.where(off, a, 0.0)) for t, a in zip(ts, mats)]
        ts = [t - _mm(hf, t) for t, hf in zip(ts, half)]
    return ts


def _softplus(x):
    return jnp.maximum(x, 0.0) + jnp.log1p(jnp.exp(-jnp.abs(x)))


def _split3(x):
    x1 = x.astype(BF16)
    r = x - x1.astype(F32)
    x2 = r.astype(BF16)
    x3 = (r - x2.astype(F32)).astype(BF16)
    return x1, x2, x3


def _ones_left(ones, x):
    p1, p2, p3 = _split3(x)
    return _dot(ones, p1) + _dot(ones, p2) + _dot(ones, p3)


def _ones_right(x, ones):
    p1, p2, p3 = _split3(x)
    return _dot(p1, ones) + _dot(p2, ones) + _dot(p3, ones)


def _dnprep_kernel(main_ref, prev_ref, next_ref, ba_ref, bat_ref, cw_ref, arow_ref, acol_ref, dtrow_ref, dtcol_ref,
                   p1_ref, p2_ref, u_ref, eg_ref, ext_ref, nat_ref, *, nb, lat_chunks, ctx_chunks, n_lat):
    c = DN_CHUNK
    rows = nb * c
    g = pl.program_id(0)

    n_slab = CONV_CH // LANES
    for sl in range(n_slab):
        cols = slice(sl * LANES, (sl + 1) * LANES)
        ext_ref[sl, 0:SUBLANES, :] = prev_ref[:, cols]
        ext_ref[sl, SUBLANES:SUBLANES + rows, :] = main_ref[:, cols]
        ext_ref[sl, SUBLANES + rows:2 * SUBLANES + rows, :] = next_ref[:, cols]
    phases = c // SUBLANES
    pad = DN_CONV // 2
    sub = lax.broadcasted_iota(jnp.int32, (SUBLANES, CONV_CH), 0)
    zcache = {}

    def phase_rows(n, t):
        if (n, t) not in zcache:
            v = jnp.concatenate(
                [ext_ref[sl, pl.ds(SUBLANES + n * c + t, SUBLANES, stride=phases), :] for sl in range(n_slab)], axis=1)
            gc = g * nb + n
            pos = jnp.where(gc < n_lat, gc % lat_chunks, (gc - n_lat) % ctx_chunks)
            if t < 0:
                v = jnp.where((sub == 0) & (pos == 0), 0.0, v)
            elif t >= phases:
                last = jnp.where(gc < n_lat, lat_chunks, ctx_chunks) - 1
                v = jnp.where((sub == SUBLANES - 1) & (pos == last), 0.0, v)
            zcache[n, t] = v
        return zcache[n, t]

    def conv_phase(n, j):
        acc = phase_rows(n, j) * cw_ref[pad]
        for k in range(DN_CONV):
            if k != pad:
                acc = acc + phase_rows(n, j + k - pad) * cw_ref[k]
        yj = _silu(acc)
        parts = []
        for hh in range(2 * HEADS):
            s = yj[:, hh * DK:(hh + 1) * DK]
            scale = (DK ** -0.5) if hh < HEADS else 1.0
            parts.append(s * (lax.rsqrt(jnp.sum(s * s, axis=-1, keepdims=True) + EPS) * scale))
        for sl in range(2 * HEADS, n_slab):
            parts.append(yj[:, sl * LANES:(sl + 1) * LANES])
        for sl in range(n_slab):
            nat_ref[sl, pl.ds(n * c + j, SUBLANES, stride=phases), :] = parts[sl]

    ri = lax.broadcasted_iota(jnp.int32, (c, c), 0)
    ci = lax.broadcasted_iota(jnp.int32, (c, c), 1)
    eye = (ri == ci).astype(F32)
    masks = _tri_masks(ri, ci, c)
    lower16 = jnp.where(ri >= ci, 1.0, 0.0).astype(BF16)
    upper16 = jnp.where(ri <= ci, 1.0, 0.0).astype(BF16)
    tri_col = jnp.concatenate([lower16, upper16], axis=0)
    tri_row = jnp.concatenate([upper16, lower16], axis=1)
    qs, ks, vs, kts, qks = {}, {}, {}, {}, {}

    def build_items(n):
        items = []
        tok = slice(n * c, (n + 1) * c)
        ba = ba_ref[tok, :]
        sig = jax.nn.sigmoid(ba)
        g_col = -jnp.exp(arow_ref[...]) * _softplus(ba + dtrow_ref[...])
        cs = _ones_left(tri_col, g_col)
        gc_col = (cs[:c], cs[c:])
        g_row = -jnp.exp(acol_ref[...]) * _softplus(bat_ref[:, tok] + dtcol_ref[...])
        rs = _ones_right(g_row, tri_row)
        gc_row = (rs[:, :c], rs[:, c:])
        for h in range(HEADS):
            q = nat_ref[h, tok, :]
            k = nat_ref[HEADS + h, tok, :]
            kb16 = k.astype(BF16)
            qs[n, h] = q
            ks[n, h] = k
            vs[n, h] = jnp.concatenate([nat_ref[2 * HEADS + 2 * h, tok, :], nat_ref[2 * HEADS + 2 * h + 1, tok, :]],
                                       axis=1)
            kts[n, h] = k.T
            qks[n, h] = _dot_nt(q.astype(BF16), kb16)
            kk = _dot_nt(kb16, kb16)
            for d in range(2):
                lane = 2 * HEADS + d * HEADS + h
                gcol = gc_col[d][:, lane:lane + 1]
                grow = gc_row[d][lane:lane + 1, :]
                incl = (ri >= ci) if d == 0 else (ri <= ci)
                strict = (ri > ci) if d == 0 else (ri < ci)
                decay = jnp.where(incl, jnp.exp(jnp.where(incl, gcol - grow, 0.0)), 0.0)
                beta = sig[:, d * HEADS + h:d * HEADS + h + 1]
                a = jnp.where(strict, beta * kk * decay, 0.0)
                items.append((n, h, d, gcol, grow, decay, beta, a))
        return items

    def emit(item, tmat):
        n, h, d, gcol, grow, decay, beta, _ = item
        hd = d * HEADS + h
        q, k, v = qs[n, h], ks[n, h], vs[n, h]
        egc = jnp.exp(gcol)
        rhs = jnp.concatenate([v * beta, k * (beta * egc)], axis=1).astype(BF16)
        uw = _dot(tmat.astype(BF16), rhs)
        glast = grow[:, c - 1:c] if d == 0 else grow[:, 0:1]
        u_ref[n, hd] = uw[:, :DV].astype(BF16)
        p1_ref[n, hd, 0:c, :] = uw[:, DV:].astype(BF16)
        p1_ref[n, hd, c:2 * c, :] = (q * egc).astype(BF16)
        p2_ref[n, hd, 0:c, :] = (qks[n, h] * decay).astype(BF16)
        p2_ref[n, hd, c:2 * c, :] = (kts[n, h] * jnp.exp(glast - grow)).astype(BF16)
        eg_ref[n, hd] = jnp.broadcast_to(jnp.exp(glast), (1, DV))

    for n in range(nb):
        for j in range(phases):
            conv_phase(n, j)
    items = [it for n in range(nb) for it in build_items(n)]
    for it, tmat in zip(items, _unit_tri_inverses([it[7] for it in items], eye, masks)):
        emit(it, tmat)


def _dnprep(dpre, ba, bat, conv_w, a_log, dt_bias, batch, l, lc):
    nt = dpre.shape[0]
    c = DN_CHUNK
    nch = nt // c
    nblk8 = nt // SUBLANES
    flat_a = jnp.concatenate([jnp.zeros((2 * HEADS,), F32), a_log.reshape(-1).astype(F32)])
    flat_dt = jnp.concatenate([jnp.zeros((2 * HEADS,), F32), dt_bias.reshape(-1).astype(F32)])
    arow = jnp.pad(flat_a, (0, LANES - N_BA)).reshape(1, LANES)
    dtrow = jnp.pad(flat_dt, (0, LANES - N_BA)).reshape(1, LANES)
    acol = flat_a.reshape(N_BA, 1)
    dtcol = flat_dt.reshape(N_BA, 1)
    cw8 = jnp.broadcast_to(conv_w.astype(F32)[:, None, :], (DN_CONV, SUBLANES, CONV_CH))
    hd = 2 * HEADS
    nb = math.gcd(DN_PREP_CHUNKS, l // c, batch * lc // c)
    rows = nb * c
    per = rows // SUBLANES
    return pl.pallas_call(
        functools.partial(_dnprep_kernel, nb=nb, lat_chunks=l // c, ctx_chunks=lc // c, n_lat=batch * l // c),
        grid=(nch // nb,),
        in_specs=[
            pl.BlockSpec((rows, CONV_CH), lambda g: (g, 0)),
            pl.BlockSpec((SUBLANES, CONV_CH), lambda g: (jnp.maximum(g * per - 1, 0), 0)),
            pl.BlockSpec((SUBLANES, CONV_CH), lambda g: (jnp.minimum((g + 1) * per, nblk8 - 1), 0)),
            pl.BlockSpec((rows, LANES), lambda g: (g, 0)),
            pl.BlockSpec((N_BA, rows), lambda g: (0, g)),
            _const_spec((DN_CONV, SUBLANES, CONV_CH)),
            _const_spec((1, LANES)),
            _const_spec((N_BA, 1)),
            _const_spec((1, LANES)),
            _const_spec((N_BA, 1)),
        ],
        out_specs=[
            pl.BlockSpec((nb, hd, 2 * c, DK), lambda g: (g, 0, 0, 0)),
            pl.BlockSpec((nb, hd, 2 * c, c), lambda g: (g, 0, 0, 0)),
            pl.BlockSpec((nb, hd, c, DV), lambda g: (g, 0, 0, 0)),
            pl.BlockSpec((nb, hd, 1, DV), lambda g: (g, 0, 0, 0)),
        ],
        out_shape=[
            jax.ShapeDtypeStruct((nch, hd, 2 * c, DK), BF16),
            jax.ShapeDtypeStruct((nch, hd, 2 * c, c), BF16),
            jax.ShapeDtypeStruct((nch, hd, c, DV), BF16),
            jax.ShapeDtypeStruct((nch, hd, 1, DV), F32),
        ],
        scratch_shapes=[pltpu.VMEM((CONV_CH // LANES, rows + 2 * SUBLANES, LANES), F32),
                        pltpu.VMEM((CONV_CH // LANES, rows, LANES), F32)],
        compiler_params=_params(("parallel",)),
        name="dnprep",
    )(dpre, dpre, dpre, ba, bat, cw8, arow, acol, dtrow, dtcol)


def _dnseq_kernel(*refs, batch):
    c = DN_CHUNK
    ins = refs[:8 * batch]
    of_ref, ob_ref, s_ref = refs[8 * batch:]

    @pl.when(pl.program_id(0) == 0)
    def _():
        s_ref[...] = jnp.zeros_like(s_ref)

    chains = []
    for b in range(batch):
        for d in range(2):
            p1_ref, p2_ref, u_ref, eg_ref = ins[8 * b + 4 * d:8 * b + 4 * d + 4]
            for h in range(HEADS):
                chains.append((b, d, h, p1_ref, p2_ref, u_ref, eg_ref))
    sidx = lambda b, d, h: (b * 2 + d) * HEADS + h
    rs = [_dot(p1_ref[0, h], s_ref[sidx(b, d, h)].astype(BF16))
          for (b, d, h, p1_ref, _, _, _) in chains]
    vnews = [(u_ref[0, h].astype(F32) - r[:c]).astype(BF16)
             for (b, d, h, _, _, u_ref, _), r in zip(chains, rs)]
    r2s = [_dot(p2_ref[0, h], vnew)
           for (b, d, h, _, p2_ref, _, _), vnew in zip(chains, vnews)]
    for (b, d, h, _, _, _, eg_ref), r, r2 in zip(chains, rs, r2s):
        o_ref = of_ref if d == 0 else ob_ref
        lo = b * VW + h * DV
        o_ref[:, lo:lo + DV] = r[c:] + r2[:c]
        s_ref[sidx(b, d, h)] = s_ref[sidx(b, d, h)] * eg_ref[0, h] + r2[c:]


def _dnseq(p1, p2, u, eg, batch, l, lc):
    c = DN_CHUNK
    nc, n = lc // c, l // c

    def fwd(b, t):
        return jnp.where(t < nc, batch * n + b * nc + t, b * n + t - nc)

    def bwd(b, t):
        return jnp.where(t < nc, batch * n + b * nc + nc - 1 - t, b * n + n - 1 - (t - nc))

    in_specs, args = [], []
    for b in range(batch):
        for d, cm in enumerate((fwd, bwd)):
            for arr in (p1, p2, u, eg):
                shp = (1, HEADS) + arr.shape[2:]
                in_specs.append(pl.BlockSpec(
                    shp, functools.partial(lambda t, cm, b, d: (cm(b, t), d, 0, 0), cm=cm, b=b, d=d)))
                args.append(arr)
    return pl.pallas_call(
        functools.partial(_dnseq_kernel, batch=batch),
        grid=(nc + n,),
        in_specs=in_specs,
        out_specs=[
            pl.BlockSpec((c, batch * VW), lambda t: (jnp.where(t < nc, n + t, t - nc), 0)),
            pl.BlockSpec((c, batch * VW), lambda t: (jnp.where(t < nc, n + nc - 1 - t, n - 1 - (t - nc)), 0)),
        ],
        out_shape=[jax.ShapeDtypeStruct((l + lc, batch * VW), F32)] * 2,
        scratch_shapes=[pltpu.VMEM((batch * 2 * HEADS, DK, DV), F32)],
        compiler_params=_params(("arbitrary",)),
        name="dnseq",
    )(*args)


def _out_kernel(*refs, batch, n_lat_tiles):
    (x_ref, m_ref, ret_ref, srg_ref, sdz_ref, sga_ref, sgb_ref, dnw_ref, wr_ref, wdn_ref, wo_ref,
     of_ref, ob_ref) = refs[:13]
    ctx_refs = refs[13:13 + 2 * batch]
    out_ref = refs[13 + 2 * batch]
    x = x_ref[...]
    gate = m_ref[0, 5:6, :]
    ret = (ret_ref[...].astype(F32) * srg_ref[...].astype(F32)).astype(BF16)
    yr = _dot(ret, wr_ref[...])
    dn_lat = of_ref[...] + ob_ref[...]
    dn_ctx = jnp.concatenate([ctx_refs[2 * b][...] + ctx_refs[2 * b + 1][...] for b in range(batch)], axis=0)
    dn_o = jnp.where(pl.program_id(0) < n_lat_tiles, dn_lat, dn_ctx)
    dnw = dnw_ref[...]
    slabs = []
    for h in range(HEADS):
        o = dn_o[:, h * DV:(h + 1) * DV]
        slabs.append(o * lax.rsqrt(jnp.mean(o * o, axis=-1, keepdims=True) + EPS) * dnw)
    dn = (jnp.concatenate(slabs, axis=1) * sdz_ref[...].astype(F32)).astype(BF16)
    yd = _dot(dn, wdn_ref[...])
    y = sga_ref[...].astype(F32) * yr + sgb_ref[...].astype(F32) * yd
    out_ref[...] = x + gate * _dot(y.astype(BF16), wo_ref[...])


def _mixer_out(xs, mods, ret, srg, sdz, sga, sgb, dn_norm_w, w_ret, w_dn, w_o, o_f, o_b,
               tm, tiles_per_batch, batch, l, lc):
    nt = xs.shape[0]
    n_lat_tiles = tiles_per_batch * batch
    assert nt // tm == n_lat_tiles + 1 and batch * lc == tm
    row = lambda w: pl.BlockSpec((tm, w), lambda i: (i, 0))
    lat = pl.BlockSpec((tm, VW), lambda i: (jnp.where(i < n_lat_tiles, i % tiles_per_batch, 0),
                                            jnp.where(i < n_lat_tiles, i // tiles_per_batch, 0)))
    ctx_specs, ctx_args = [], []
    for b in range(batch):
        for arr in (o_f, o_b):
            ctx_specs.append(pl.BlockSpec((lc, VW), functools.partial(lambda i, b: (l // lc, b), b=b)))
            ctx_args.append(arr)
    return pl.pallas_call(
        functools.partial(_out_kernel, batch=batch, n_lat_tiles=n_lat_tiles),
        grid=(nt // tm,),
        in_specs=[
            row(D_MODEL), _mod_spec(tiles_per_batch, batch),
            row(VW), row(VW), row(VW), row(D_MODEL), row(D_MODEL),
            _const_spec((1, DV)),
            _const_spec((VW, D_MODEL)), _const_spec((VW, D_MODEL)), _const_spec((D_MODEL, D_MODEL)),
            lat, lat,
        ] + ctx_specs,
        out_specs=row(D_MODEL),
        out_shape=jax.ShapeDtypeStruct((nt, D_MODEL), F32),
        compiler_params=_params(("parallel",)),
        name="mixer_out",
    )(xs, mods, ret, srg, sdz, sga, sgb, dn_norm_w.reshape(1, DV).astype(F32), w_ret, w_dn, w_o, o_f, o_b,
      *ctx_args)


def _rope_tables(l, tm):
    half = DK // 2
    n_freq = half // 2
    inv = ROPE_BASE ** (-jnp.arange(n_freq, dtype=F32) / n_freq)
    rows = l // GRID_W
    ang_r = jnp.arange(rows, dtype=F32)[:, None] * inv
    ang_c = jnp.arange(GRID_W, dtype=F32)[:, None] * inv
    ang = jnp.concatenate([
        jnp.broadcast_to(ang_r[:, None, :], (rows, GRID_W, n_freq)),
        jnp.broadcast_to(ang_c[None, :, :], (rows, GRID_W, n_freq)),
    ], axis=-1).reshape(l, half)
    cos, sin = jnp.cos(ang), jnp.sin(ang)
    cos_t = jnp.concatenate([jnp.concatenate([cos, cos], axis=1), jnp.ones((tm, DK), F32)], axis=0)
    sin_t = jnp.concatenate([jnp.concatenate([-sin, sin], axis=1), jnp.zeros((tm, DK), F32)], axis=0)
    return cos_t, sin_t


def kernel(x, c, ctx, c_ctx, ada_w, ada_b, norm_w, ffn1_wgu, ffn1_wd, w_in, dn_conv_w, dn_a_log, dn_dt_bias,
           dn_norm_w, w_ret_out, w_dn_out, w_o, ffn2_wgu, ffn2_wd, final_norm_w):
    batch, l, _ = x.shape
    lc = ctx.shape[1]
    depth = ada_w.shape[0]
    assert batch + 1 <= SUBLANES and l % DN_CHUNK == 0 and lc % DN_CHUNK == 0 and l % GRID_W == 0
    tm = _row_tile(l, batch * lc)
    tiles_per_batch = l // tm

    assert batch * lc == tm
    xs = x.reshape(batch * l, D_MODEL)
    ctx_rows = ctx.reshape(batch * lc, D_MODEL)
    cmat =jnp.zeros((SUBLANES, D_MODEL), F32).at[:batch].set(c).at[batch].set(c_ctx)
    mods_all = _ada_all(cmat, ada_w, ada_b).reshape(depth, SUBLANES, N_MOD, D_MODEL)
    cos_t, sin_t = _rope_tables(l, tm)
    log_gamma = jnp.log1p(-jnp.power(2.0, -5.0 - jnp.arange(HEADS, dtype=F32)))
    log_gamma_t = jnp.broadcast_to(log_gamma[:, None, None], (HEADS, 1, LANES))

    for layer in range(depth):
        mods = mods_all[layer, :batch + 1]
        w_l = w_in[layer]
        w_main = jnp.concatenate([w_l[:, :_OFF_GA], w_l[:, _OFF_GA + N_BA:]], axis=1).astype(BF16)
        w_ba_cols = w_l[:, _OFF_GA:_OFF_GA + N_BA]
        w_ba = jnp.pad(w_ba_cols, ((0, 0), (0, LANES - N_BA))).astype(BF16)
        w_bat = w_ba_cols.T.astype(BF16)

        xs = _ffn(xs, mods, norm_w[layer, 0], ffn1_wgu[layer].astype(BF16), ffn1_wd[layer].astype(BF16),
                  final_norm_w, 0, tm, tiles_per_batch, batch, ctx_rows=ctx_rows if layer == 0 else None)
        rq, rk, rv, srg, dpre, sdz, ba, bat, sga, sgb = _inproj(
            xs, mods, norm_w[layer, 1], w_main, w_ba, w_bat, cos_t, sin_t, tm, tiles_per_batch, batch)
        ret = _retention(rq, rk, rv, log_gamma_t, batch, l, lc)
        p1, p2, u, eg = _dnprep(dpre, ba, bat, dn_conv_w[layer], dn_a_log[layer], dn_dt_bias[layer], batch, l, lc)
        o_f, o_b = _dnseq(p1, p2, u, eg, batch, l, lc)
        xs = _mixer_out(xs, mods, ret, srg, sdz, sga, sgb, dn_norm_w[layer], w_ret_out[layer].astype(BF16),
                        w_dn_out[layer].astype(BF16), w_o[layer].astype(BF16), o_f, o_b,
                        tm, tiles_per_batch, batch, l, lc)
        xs = _ffn(xs, mods, norm_w[layer, 2], ffn2_wgu[layer].astype(BF16), ffn2_wd[layer].astype(BF16),
                  final_norm_w, 6, tm, tiles_per_batch, batch, final=layer == depth - 1)

    return xs.reshape(batch, l, D_MODEL)
```

```python
import functools
import math

import jax
import jax.numpy as jnp
from jax import lax
from jax.experimental import pallas as pl
from jax.experimental.pallas import tpu as pltpu

F32 = jnp.float32
BF16 = jnp.bfloat16

D_MODEL = 1024
FFN_HIDDEN = 2816
N_MOD = 9
GRID_W = 64
ROPE_BASE = 10000.0
EPS = 1e-6

HEADS = 4
DK = 128
DV = 256
QK = HEADS * DK
VW = HEADS * DV
DN_CONV = 5
CONV_CH = 2 * QK + VW
N_BA = 4 * HEADS

LANES = 128
SUBLANES = 8
MXU_N = 256
VMEM_LIMIT = 56 * 1024 * 1024

DN_CHUNK = 128
DN_PREP_CHUNKS = 2
FFN_CHUNK = MXU_N


def _dot(a, b):
    return jnp.dot(a, b, preferred_element_type=F32)


def _dot_nt(a, b):
    return lax.dot_general(a, b, (((1,), (1,)), ((), ())), preferred_element_type=F32)


def _silu(x):
    return x * jax.nn.sigmoid(x)


def _row_tile(l, lc_total):
    for t in (512, 256, 128):
        if l % t == 0 and lc_total % t == 0:
            return t
    raise ValueError("sequence lengths must be multiples of 128")


def _params(sem):
    return pltpu.CompilerParams(dimension_semantics=sem, vmem_limit_bytes=VMEM_LIMIT)


def _const_spec(shape):
    nd = len(shape)
    return pl.BlockSpec(shape, lambda *_: (0,) * nd, pipeline_mode=pl.Buffered(1))


def _ada_kernel(c_ref, w_ref, b_ref, o_ref):
    c = c_ref[...]
    s = _silu(c)
    o_ref[0] = jnp.dot(s, w_ref[0], preferred_element_type=F32,
                       precision=lax.Precision.HIGHEST) + b_ref[0]


def _ada_all(cmat, ada_w, ada_b):
    depth = ada_w.shape[0]
    n = ada_w.shape[2]
    tn = 1024
    return pl.pallas_call(
        _ada_kernel,
        grid=(depth, n // tn),
        in_specs=[
            pl.BlockSpec((SUBLANES, D_MODEL), lambda l, j: (0, 0)),
            pl.BlockSpec((1, D_MODEL, tn), lambda l, j: (l, 0, j)),
            pl.BlockSpec((1, 1, tn), lambda l, j: (l, 0, j)),
        ],
        out_specs=pl.BlockSpec((1, SUBLANES, tn), lambda l, j: (l, 0, j)),
        out_shape=jax.ShapeDtypeStruct((depth, SUBLANES, n), F32),
        compiler_params=_params(("parallel", "parallel")),
        name="ada",
    )(cmat, ada_w, ada_b.reshape(depth, 1, n))


def _norm_mod(x, nw, shift, scale):
    ms = jnp.mean(x * x, axis=-1, keepdims=True)
    h = x * lax.rsqrt(ms + EPS) * nw
    return h * (1.0 + scale) + shift


def _mod_spec(tiles_per_batch, batch):
    return pl.BlockSpec((1, N_MOD, D_MODEL), lambda i: (jnp.minimum(i // tiles_per_batch, batch), 0, 0))


def _ffn_kernel(x_ref, c_ref, m_ref, nw_ref, wgu_ref, wd_ref, fw_ref, o_ref, *, k0, final, n_lat_tiles):
    x = x_ref[...] if n_lat_tiles is None else jnp.where(pl.program_id(0) < n_lat_tiles, x_ref[...], c_ref[...])
    shift = m_ref[0, k0:k0 + 1, :]
    scale = m_ref[0, k0 + 1:k0 + 2, :]
    gate = m_ref[0, k0 + 2:k0 + 3, :]
    h = _norm_mod(x, nw_ref[...], shift, scale).astype(BF16)
    acc = jnp.zeros(x.shape, F32)
    for c in range(FFN_HIDDEN // FFN_CHUNK):
        lo = c * FFN_CHUNK
        g = _dot(h, wgu_ref[:, lo:lo + FFN_CHUNK])
        u = _dot(h, wgu_ref[:, FFN_HIDDEN + lo:FFN_HIDDEN + lo + FFN_CHUNK])
        a = (_silu(g) * u).astype(BF16)
        acc = acc + _dot(a, wd_ref[lo:lo + FFN_CHUNK, :])
    out = x + (0.5 * gate) * acc
    if final:
        out = out * lax.rsqrt(jnp.mean(out * out, axis=-1, keepdims=True) + EPS) * fw_ref[...]
    o_ref[...] = out


def _ffn(xs, mods, nw, wgu, wd, final_w, k0, tm, tiles_per_batch, batch, final=False, ctx_rows=None):
    n_lat_tiles = tiles_per_batch * batch
    if ctx_rows is None:
        rows = n_lat_tiles * tm if final else xs.shape[0]
        x_spec = pl.BlockSpec((tm, D_MODEL), lambda i: (i, 0))
        c_arg, c_spec = final_w.reshape(1, D_MODEL), _const_spec((1, D_MODEL))
    else:
        assert ctx_rows.shape[0] == tm and not final
        rows = (n_lat_tiles + 1) * tm
        x_spec = pl.BlockSpec((tm, D_MODEL), lambda i: (jnp.minimum(i, n_lat_tiles - 1), 0))
        c_arg, c_spec = ctx_rows, _const_spec((tm, D_MODEL))
    return pl.pallas_call(
        functools.partial(_ffn_kernel, k0=k0, final=final, n_lat_tiles=None if ctx_rows is None else n_lat_tiles),
        grid=(rows // tm,),
        in_specs=[
            x_spec,
            c_spec,
            _mod_spec(tiles_per_batch, batch),
            _const_spec((1, D_MODEL)),
            _const_spec((D_MODEL, 2 * FFN_HIDDEN)),
            _const_spec((FFN_HIDDEN, D_MODEL)),
            _const_spec((1, D_MODEL)),
        ],
        out_specs=pl.BlockSpec((tm, D_MODEL), lambda i: (i, 0)),
        out_shape=jax.ShapeDtypeStruct((rows, D_MODEL), F32),
        compiler_params=_params(("parallel",)),
        name="ffn",
    )(xs, c_arg, mods, nw.reshape(1, D_MODEL), wgu, wd, final_w.reshape(1, D_MODEL))


_OFF_RQ, _OFF_RK, _OFF_RV, _OFF_RG = 0, QK, 2 * QK, 2 * QK + VW
_OFF_DQKV = 2 * QK + 2 * VW
_OFF_DZ = _OFF_DQKV + CONV_CH
_OFF_GA = _OFF_DZ + VW
_OFF_GB = _OFF_GA + D_MODEL
_W_MAIN = _OFF_GB + D_MODEL


def _inproj_kernel(x_ref, m_ref, nw_ref, w_ref, wba_ref, wbat_ref, cos_ref, sin_ref,
                   rq_ref, rk_ref, rv_ref, srg_ref, dpre_ref, sdz_ref, ba_ref, bat_ref, sga_ref, sgb_ref):
    h = _norm_mod(x_ref[...], nw_ref[...], m_ref[0, 3:4, :], m_ref[0, 4:5, :]).astype(BF16)
    cos = cos_ref[...]
    sin = sin_ref[...]
    blk = 2 * DK

    def rope_job(o_ref, off, scale, j):
        def run():
            p = _dot(h, w_ref[:, off + j * blk:off + (j + 1) * blk]) * scale
            outs = []
            for hh in range(blk // DK):
                slab = p[:, hh * DK:(hh + 1) * DK]
                outs.append(slab * cos + pltpu.roll(slab, DK // 2, 1) * sin)
            o_ref[:, j * blk:(j + 1) * blk] = jnp.concatenate(outs, axis=1).astype(BF16)
        return run

    def act_job(o_ref, off, act, j):
        def run():
            o_ref[:, j * blk:(j + 1) * blk] = act(_dot(h, w_ref[:, off + j * blk:off + (j + 1) * blk])).astype(BF16)
        return run

    def pre_job(j):
        def run():
            dpre_ref[:, j * blk:(j + 1) * blk] = _dot(h, w_ref[:, _OFF_DQKV + j * blk:_OFF_DQKV + (j + 1) * blk])
        return run

    jobs = [rope_job(rq_ref, _OFF_RQ, 1.0, j) for j in range(QK // blk)]
    jobs += [rope_job(rk_ref, _OFF_RK, DK ** -0.5, j) for j in range(QK // blk)]
    jobs += [act_job(rv_ref, _OFF_RV, lambda p: p, j) for j in range(VW // blk)]
    jobs += [act_job(srg_ref, _OFF_RG, _silu, j) for j in range(VW // blk)]
    jobs += [act_job(sdz_ref, _OFF_DZ, _silu, j) for j in range(VW // blk)]
    jobs += [act_job(sga_ref, _OFF_GA, jax.nn.sigmoid, j) for j in range(D_MODEL // blk)]
    jobs += [act_job(sgb_ref, _OFF_GB, jax.nn.sigmoid, j) for j in range(D_MODEL // blk)]
    jobs += [pre_job(j) for j in range(CONV_CH // blk)]
    for job in jobs:
        job()
    ba_ref[...] = _dot(h, wba_ref[...])
    bat_ref[...] = _dot_nt(wbat_ref[...], h)


def _inproj(xs, mods, nw, w_main, w_ba, w_bat, cos_t, sin_t, tm, tiles_per_batch, batch):
    nt = xs.shape[0]
    n_lat_tiles = tiles_per_batch * batch

    def rope_map(i):
        return (jnp.where(i < n_lat_tiles, i % tiles_per_batch, tiles_per_batch), 0)

    row = lambda w: pl.BlockSpec((tm, w), lambda i: (i, 0))
    outs = [
        (QK, BF16), (QK, BF16), (VW, BF16), (VW, BF16), (CONV_CH, F32), (VW, BF16),
        (LANES, F32), None, (D_MODEL, BF16), (D_MODEL, BF16),
    ]
    out_specs, out_shape = [], []
    for o in outs:
        if o is None:
            out_specs.append(pl.BlockSpec((N_BA, tm), lambda i: (0, i)))
            out_shape.append(jax.ShapeDtypeStruct((N_BA, nt), F32))
        else:
            out_specs.append(row(o[0]))
            out_shape.append(jax.ShapeDtypeStruct((nt, o[0]), o[1]))
    return pl.pallas_call(
        _inproj_kernel,
        grid=(nt // tm,),
        in_specs=[
            row(D_MODEL),
            _mod_spec(tiles_per_batch, batch),
            _const_spec((1, D_MODEL)),
            _const_spec((D_MODEL, _W_MAIN)),
            _const_spec((D_MODEL, LANES)),
            _const_spec((N_BA, D_MODEL)),
            pl.BlockSpec((tm, DK), rope_map),
            pl.BlockSpec((tm, DK), rope_map),
        ],
        out_specs=out_specs,
        out_shape=out_shape,
        compiler_params=_params(("parallel",)),
        name="inproj",
    )(xs, mods, nw.reshape(1, D_MODEL), w_main, w_ba, w_bat, cos_t, sin_t)


def _ret_kernel(lg_ref, q_ref, k_ref, v_ref, o_ref,
                sf_ref, sb_ref, sball_ref, dm_ref, eq_ref, ekf_ref, ekb_ref, *, nc, n, c):
    ph = pl.program_id(1)
    t = pl.program_id(2)

    @pl.when((ph == 0) & (t == 0))
    def _():
        sb_ref[...] = jnp.zeros_like(sb_ref)
        ii = lax.broadcasted_iota(jnp.int32, (c, 1), 0).astype(F32)
        jj = lax.broadcasted_iota(jnp.int32, (1, c), 1).astype(F32)
        for h in range(HEADS):
            lg = lg_ref[h]
            dm_ref[h] = jnp.exp(lg[:, :1] * jnp.abs(ii - jj))
            eq_ref[h, :, :DK] = jnp.exp(lg * (ii + 1.0))
            eq_ref[h, :, DK:] = jnp.exp(lg * (float(c) - ii))
            ekf_ref[h] = jnp.exp(lg * (float(c) - 1.0 - ii))
            ekb_ref[h] = jnp.exp(lg * ii)

    def kv_update(h, s_ref, ek_ref):
        k = k_ref[:, h * DK:(h + 1) * DK].astype(F32) * ek_ref[h]
        kv = _dot(k.T.astype(BF16), v_ref[:, h * DV:(h + 1) * DV])
        gamma_c = jnp.exp(lg_ref[h][:, :1] * float(c))
        s_ref[h] = s_ref[h] * gamma_c + kv

    @pl.when(ph == 0)
    def _():
        pos = jnp.where(t < nc, nc - 1 - t, nc + n - 1 - (t - nc))
        for h in range(HEADS):
            sball_ref[pos, h] = sb_ref[h].astype(BF16)
            kv_update(h, sb_ref, ekb_ref)

    @pl.when(ph == 1)
    def _():
        @pl.when(t == 0)
        def _():
            sf_ref[...] = jnp.zeros_like(sf_ref)
        for h in range(HEADS):
            q = q_ref[:, h * DK:(h + 1) * DK]
            qf = q.astype(F32)
            s = _dot_nt(q, k_ref[:, h * DK:(h + 1) * DK]) * dm_ref[h]
            o = _dot(s.astype(BF16), v_ref[:, h * DV:(h + 1) * DV])
            q2 = (jnp.concatenate([qf, qf], axis=1) * eq_ref[h]).astype(BF16)
            s2 = jnp.concatenate([sf_ref[h].astype(BF16), sball_ref[t, h]], axis=0)
            o = o + _dot(q2, s2)
            o = o * lax.rsqrt(jnp.mean(o * o, axis=-1, keepdims=True) + EPS)
            o_ref[:, h * DV:(h + 1) * DV] = o.astype(BF16)
            kv_update(h, sf_ref, ekf_ref)


def _retention(rq, rk, rv, log_gamma_t, batch, l, lc):
    nt = rq.shape[0]
    c = min(256, lc)
    nc, n = lc // c, l // c
    lat0 = lambda b: b * n
    ctx0 = lambda b: batch * n + b * nc

    def chunk(b, ph, t):
        fwd = jnp.where(t < nc, ctx0(b) + t, lat0(b) + t - nc)
        bwd = jnp.where(t < nc, ctx0(b) + nc - 1 - t, lat0(b) + n - 1 - (t - nc))
        return jnp.where(ph == 0, bwd, fwd)

    def q_chunk(b, ph, t):
        return jnp.where(ph == 0, ctx0(b), chunk(b, ph, t))

    return pl.pallas_call(
        functools.partial(_ret_kernel, nc=nc, n=n, c=c),
        grid=(batch, 2, nc + n),
        in_specs=[
            _const_spec((HEADS, 1, LANES)),
            pl.BlockSpec((c, QK), lambda b, ph, t: (q_chunk(b, ph, t), 0)),
            pl.BlockSpec((c, QK), lambda b, ph, t: (chunk(b, ph, t), 0)),
            pl.BlockSpec((c, VW), lambda b, ph, t: (chunk(b, ph, t), 0)),
        ],
        out_specs=pl.BlockSpec((c, VW), lambda b, ph, t: (q_chunk(b, ph, t), 0)),
        out_shape=jax.ShapeDtypeStruct((nt, VW), BF16),
        scratch_shapes=[
            pltpu.VMEM((HEADS, DK, DV), F32),
            pltpu.VMEM((HEADS, DK, DV), F32),
            pltpu.VMEM((nc + n, HEADS, DK, DV), BF16),
            pltpu.VMEM((HEADS, c, c), F32),
            pltpu.VMEM((HEADS, c, 2 * DK), F32),
            pltpu.VMEM((HEADS, c, DK), F32),
            pltpu.VMEM((HEADS, c, DK), F32),
        ],
        compiler_params=_params(("parallel", "arbitrary", "arbitrary")),
        name="retention",
    )(log_gamma_t, rq, rk, rv)


def _mm(a, b):
    return _dot(a.astype(BF16), b.astype(BF16))


TRI_BASE = 8


def _tri_masks(ri, ci, c):
    diag = (ri // TRI_BASE) == (ci // TRI_BASE)
    offs = []
    s = TRI_BASE
    while s < c:
        offs.append(((ri // s) ^ (ci // s)) == 1)
        s *= 2
    return diag, offs


def _unit_tri_inverses(mats, eye, masks):
    diag, offs = masks
    qs = [jnp.where(diag, a, 0.0) for a in mats]
    ts = [eye - q for q in qs]
    for _ in range(int(math.log2(TRI_BASE)) - 1):
        qs = [_mm(q, q) for q in qs]
        ts = [t + _mm(t, q) for t, q in zip(ts, qs)]
    for off in offs:
        half = [_mm(t, jnp.where(off, a, 0.0)) for t, a in zip(ts, mats)]
        ts = [t - _mm(hf, t) for t, hf in zip(ts, half)]
    return ts


def _softplus(x):
    return jnp.maximum(x, 0.0) + jnp.log1p(jnp.exp(-jnp.abs(x)))


def _split3(x):
    x1 = x.astype(BF16)
    r = x - x1.astype(F32)
    x2 = r.astype(BF16)
    x3 = (r - x2.astype(F32)).astype(BF16)
    return x1, x2, x3


def _ones_left(ones, x):
    p1, p2, p3 = _split3(x)
    return _dot(ones, p1) + _dot(ones, p2) + _dot(ones, p3)


def _ones_right(x, ones):
    p1, p2, p3 = _split3(x)
    return _dot(p1, ones) + _dot(p2, ones) + _dot(p3, ones)


def _dnprep_kernel(main_ref, prev_ref, next_ref, ba_ref, bat_ref, cw_ref, arow_ref, acol_ref, dtrow_ref, dtcol_ref,
                   p1_ref, p2_ref, u_ref, eg_ref, ext_ref, nat_ref, *, nb, lat_chunks, ctx_chunks, n_lat):
    c = DN_CHUNK
    rows = nb * c
    g = pl.program_id(0)

    n_slab = CONV_CH // LANES
    for sl in range(n_slab):
        cols = slice(sl * LANES, (sl + 1) * LANES)
        ext_ref[sl, 0:SUBLANES, :] = prev_ref[:, cols]
        ext_ref[sl, SUBLANES:SUBLANES + rows, :] = main_ref[:, cols]
        ext_ref[sl, SUBLANES + rows:2 * SUBLANES + rows, :] = next_ref[:, cols]
    phases = c // SUBLANES
    pad = DN_CONV // 2
    sub = lax.broadcasted_iota(jnp.int32, (SUBLANES, CONV_CH), 0)
    zcache = {}

    def phase_rows(n, t):
        if (n, t) not in zcache:
            v = jnp.concatenate(
                [ext_ref[sl, pl.ds(SUBLANES + n * c + t, SUBLANES, stride=phases), :] for sl in range(n_slab)], axis=1)
            gc = g * nb + n
            pos = jnp.where(gc < n_lat, gc % lat_chunks, (gc - n_lat) % ctx_chunks)
            if t < 0:
                v = jnp.where((sub == 0) & (pos == 0), 0.0, v)
            elif t >= phases:
                last = jnp.where(gc < n_lat, lat_chunks, ctx_chunks) - 1
                v = jnp.where((sub == SUBLANES - 1) & (pos == last), 0.0, v)
            zcache[n, t] = v
        return zcache[n, t]

    def conv_phase(n, j):
        acc = phase_rows(n, j) * cw_ref[pad]
        for k in range(DN_CONV):
            if k != pad:
                acc = acc + phase_rows(n, j + k - pad) * cw_ref[k]
        yj = _silu(acc)
        parts = []
        for hh in range(2 * HEADS):
            s = yj[:, hh * DK:(hh + 1) * DK]
            scale = (DK ** -0.5) if hh < HEADS else 1.0
            parts.append(s * (lax.rsqrt(jnp.sum(s * s, axis=-1, keepdims=True) + EPS) * scale))
        for sl in range(2 * HEADS, n_slab):
            parts.append(yj[:, sl * LANES:(sl + 1) * LANES])
        for sl in range(n_slab):
            nat_ref[sl, pl.ds(n * c + j, SUBLANES, stride=phases), :] = parts[sl]

    ri = lax.broadcasted_iota(jnp.int32, (c, c), 0)
    ci = lax.broadcasted_iota(jnp.int32, (c, c), 1)
    eye = (ri == ci).astype(F32)
    masks = _tri_masks(ri, ci, c)
    lower16 = jnp.where(ri >= ci, 1.0, 0.0).astype(BF16)
    upper16 = jnp.where(ri <= ci, 1.0, 0.0).astype(BF16)
    tri_col = jnp.concatenate([lower16, upper16], axis=0)
    tri_row = jnp.concatenate([upper16, lower16], axis=1)
    qs, ks, vs, kts, qks = {}, {}, {}, {}, {}

    def build_items(n):
        items = []
        tok = slice(n * c, (n + 1) * c)
        ba = ba_ref[tok, :]
        sig = jax.nn.sigmoid(ba)
        g_col = -jnp.exp(arow_ref[...]) * _softplus(ba + dtrow_ref[...])
        cs = _ones_left(tri_col, g_col)
        gc_col = (cs[:c], cs[c:])
        g_row = -jnp.exp(acol_ref[...]) * _softplus(bat_ref[:, tok] + dtcol_ref[...])
        rs = _ones_right(g_row, tri_row)
        gc_row = (rs[:, :c], rs[:, c:])
        for h in range(HEADS):
            q = nat_ref[h, tok, :]
            k = nat_ref[HEADS + h, tok, :]
            kb16 = k.astype(BF16)
            qs[n, h] = q
            ks[n, h] = k
            vs[n, h] = jnp.concatenate([nat_ref[2 * HEADS + 2 * h, tok, :], nat_ref[2 * HEADS + 2 * h + 1, tok, :]],
                                       axis=1)
            kts[n, h] = k.T
            qks[n, h] = _dot_nt(q.astype(BF16), kb16)
            kk = _dot_nt(kb16, kb16)
            for d in range(2):
                lane = 2 * HEADS + d * HEADS + h
                gcol = gc_col[d][:, lane:lane + 1]
                grow = gc_row[d][lane:lane + 1, :]
                incl = (ri >= ci) if d == 0 else (ri <= ci)
                strict = (ri > ci) if d == 0 else (ri < ci)
                decay = jnp.where(incl, jnp.exp(jnp.where(incl, gcol - grow, 0.0)), 0.0)
                beta = sig[:, d * HEADS + h:d * HEADS + h + 1]
                a = jnp.where(strict, beta * kk * decay, 0.0)
                items.append((n, h, d, gcol, grow, decay, beta, a))
        return items

    def emit(item, tmat):
        n, h, d, gcol, grow, decay, beta, _ = item
        hd = d * HEADS + h
        q, k, v = qs[n, h], ks[n, h], vs[n, h]
        egc = jnp.exp(gcol)
        rhs = jnp.concatenate([v * beta, k * (beta * egc)], axis=1).astype(BF16)
        uw = _dot(tmat.astype(BF16), rhs)
        glast = grow[:, c - 1:c] if d == 0 else grow[:, 0:1]
        u_ref[n, hd] = uw[:, :DV].astype(BF16)
        p1_ref[n, hd, 0:c, :] = uw[:, DV:].astype(BF16)
        p1_ref[n, hd, c:2 * c, :] = (q * egc).astype(BF16)
        p2_ref[n, hd, 0:c, :] = (qks[n, h] * decay).astype(BF16)
        p2_ref[n, hd, c:2 * c, :] = (kts[n, h] * jnp.exp(glast - grow)).astype(BF16)
        eg_ref[n, hd] = jnp.broadcast_to(jnp.exp(glast), (1, DV))

    for n in range(nb):
        for j in range(phases):
            conv_phase(n, j)
    items = [it for n in range(nb) for it in build_items(n)]
    for it, tmat in zip(items, _unit_tri_inverses([it[7] for it in items], eye, masks)):
        emit(it, tmat)


def _dnprep(dpre, ba, bat, conv_w, a_log, dt_bias, batch, l, lc):
    nt = dpre.shape[0]
    c = DN_CHUNK
    nch = nt // c
    nblk8 = nt // SUBLANES
    flat_a = jnp.concatenate([jnp.zeros((2 * HEADS,), F32), a_log.reshape(-1).astype(F32)])
    flat_dt = jnp.concatenate([jnp.zeros((2 * HEADS,), F32), dt_bias.reshape(-1).astype(F32)])
    arow = jnp.pad(flat_a, (0, LANES - N_BA)).reshape(1, LANES)
    dtrow = jnp.pad(flat_dt, (0, LANES - N_BA)).reshape(1, LANES)
    acol = flat_a.reshape(N_BA, 1)
    dtcol = flat_dt.reshape(N_BA, 1)
    cw8 = jnp.broadcast_to(conv_w.astype(F32)[:, None, :], (DN_CONV, SUBLANES, CONV_CH))
    hd = 2 * HEADS
    nb = math.gcd(DN_PREP_CHUNKS, l // c, batch * lc // c)
    rows = nb * c
    per = rows // SUBLANES
    return pl.pallas_call(
        functools.partial(_dnprep_kernel, nb=nb, lat_chunks=l // c, ctx_chunks=lc // c, n_lat=batch * l // c),
        grid=(nch // nb,),
        in_specs=[
            pl.BlockSpec((rows, CONV_CH), lambda g: (g, 0)),
            pl.BlockSpec((SUBLANES, CONV_CH), lambda g: (jnp.maximum(g * per - 1, 0), 0)),
            pl.BlockSpec((SUBLANES, CONV_CH), lambda g: (jnp.minimum((g + 1) * per, nblk8 - 1), 0)),
            pl.BlockSpec((rows, LANES), lambda g: (g, 0)),
            pl.BlockSpec((N_BA, rows), lambda g: (0, g)),
            _const_spec((DN_CONV, SUBLANES, CONV_CH)),
            _const_spec((1, LANES)),
            _const_spec((N_BA, 1)),
            _const_spec((1, LANES)),
            _const_spec((N_BA, 1)),
        ],
        out_specs=[
            pl.BlockSpec((nb, hd, 2 * c, DK), lambda g: (g, 0, 0, 0)),
            pl.BlockSpec((nb, hd, 2 * c, c), lambda g: (g, 0, 0, 0)),
            pl.BlockSpec((nb, hd, c, DV), lambda g: (g, 0, 0, 0)),
            pl.BlockSpec((nb, hd, 1, DV), lambda g: (g, 0, 0, 0)),
        ],
        out_shape=[
            jax.ShapeDtypeStruct((nch, hd, 2 * c, DK), BF16),
            jax.ShapeDtypeStruct((nch, hd, 2 * c, c), BF16),
            jax.ShapeDtypeStruct((nch, hd, c, DV), BF16),
            jax.ShapeDtypeStruct((nch, hd, 1, DV), F32),
        ],
        scratch_shapes=[pltpu.VMEM((CONV_CH // LANES, rows + 2 * SUBLANES, LANES), F32),
                        pltpu.VMEM((CONV_CH // LANES, rows, LANES), F32)],
        compiler_params=_params(("parallel",)),
        name="dnprep",
    )(dpre, dpre, dpre, ba, bat, cw8, arow, acol, dtrow, dtcol)


def _dnseq_kernel(*refs, batch):
    c = DN_CHUNK
    ins = refs[:8 * batch]
    of_ref, ob_ref, s_ref = refs[8 * batch:]

    @pl.when(pl.program_id(0) == 0)
    def _():
        s_ref[...] = jnp.zeros_like(s_ref)

    chains = []
    for b in range(batch):
        for d in range(2):
            p1_ref, p2_ref, u_ref, eg_ref = ins[8 * b + 4 * d:8 * b + 4 * d + 4]
            for h in range(HEADS):
                chains.append((b, d, h, p1_ref, p2_ref, u_ref, eg_ref))
    sidx = lambda b, d, h: (b * 2 + d) * HEADS + h
    rs = [_dot(p1_ref[0, h], s_ref[sidx(b, d, h)].astype(BF16))
          for (b, d, h, p1_ref, _, _, _) in chains]
    vnews = [(u_ref[0, h].astype(F32) - r[:c]).astype(BF16)
             for (b, d, h, _, _, u_ref, _), r in zip(chains, rs)]
    r2s = [_dot(p2_ref[0, h], vnew)
           for (b, d, h, _, p2_ref, _, _), vnew in zip(chains, vnews)]
    for (b, d, h, _, _, _, eg_ref), r, r2 in zip(chains, rs, r2s):
        o_ref = of_ref if d == 0 else ob_ref
        lo = b * VW + h * DV
        o_ref[:, lo:lo + DV] = (r[c:] + r2[:c]).astype(BF16)
        s_ref[sidx(b, d, h)] = s_ref[sidx(b, d, h)] * eg_ref[0, h] + r2[c:]


def _dnseq(p1, p2, u, eg, batch, l, lc):
    c = DN_CHUNK
    nc, n = lc // c, l // c

    def fwd(b, t):
        return jnp.where(t < nc, batch * n + b * nc + t, b * n + t - nc)

    def bwd(b, t):
        return jnp.where(t < nc, batch * n + b * nc + nc - 1 - t, b * n + n - 1 - (t - nc))

    in_specs, args = [], []
    for b in range(batch):
        for d, cm in enumerate((fwd, bwd)):
            for arr in (p1, p2, u, eg):
                shp = (1, HEADS) + arr.shape[2:]
                in_specs.append(pl.BlockSpec(
                    shp, functools.partial(lambda t, cm, b, d: (cm(b, t), d, 0, 0), cm=cm, b=b, d=d)))
                args.append(arr)
    return pl.pallas_call(
        functools.partial(_dnseq_kernel, batch=batch),
        grid=(nc + n,),
        in_specs=in_specs,
        out_specs=[
            pl.BlockSpec((c, batch * VW), lambda t: (jnp.where(t < nc, n + t, t - nc), 0)),
            pl.BlockSpec((c, batch * VW), lambda t: (jnp.where(t < nc, n + nc - 1 - t, n - 1 - (t - nc)), 0)),
        ],
        out_shape=[jax.ShapeDtypeStruct((l + lc, batch * VW), BF16)] * 2,
        scratch_shapes=[pltpu.VMEM((batch * 2 * HEADS, DK, DV), F32)],
        compiler_params=_params(("arbitrary",)),
        name="dnseq",
    )(*args)


def _out_kernel(*refs, batch, n_lat_tiles):
    (x_ref, m_ref, ret_ref, srg_ref, sdz_ref, sga_ref, sgb_ref, dnw_ref, wr_ref, wdn_ref, wo_ref,
     of_ref, ob_ref) = refs[:13]
    ctx_refs = refs[13:13 + 2 * batch]
    out_ref = refs[13 + 2 * batch]
    x = x_ref[...]
    gate = m_ref[0, 5:6, :]
    ret = (ret_ref[...].astype(F32) * srg_ref[...].astype(F32)).astype(BF16)
    yr = _dot(ret, wr_ref[...])
    dn_lat = of_ref[...].astype(F32) + ob_ref[...].astype(F32)
    dn_ctx = jnp.concatenate([ctx_refs[2 * b][...].astype(F32) + ctx_refs[2 * b + 1][...].astype(F32)
                              for b in range(batch)], axis=0)
    dn_o = jnp.where(pl.program_id(0) < n_lat_tiles, dn_lat, dn_ctx)
    dnw = dnw_ref[...]
    slabs = []
    for h in range(HEADS):
        o = dn_o[:, h * DV:(h + 1) * DV]
        slabs.append(o * lax.rsqrt(jnp.mean(o * o, axis=-1, keepdims=True) + EPS) * dnw)
    dn = (jnp.concatenate(slabs, axis=1) * sdz_ref[...].astype(F32)).astype(BF16)
    yd = _dot(dn, wdn_ref[...])
    y = sga_ref[...].astype(F32) * yr + sgb_ref[...].astype(F32) * yd
    out_ref[...] = x + gate * _dot(y.astype(BF16), wo_ref[...])


def _mixer_out(xs, mods, ret, srg, sdz, sga, sgb, dn_norm_w, w_ret, w_dn, w_o, o_f, o_b,
               tm, tiles_per_batch, batch, l, lc):
    nt = xs.shape[0]
    n_lat_tiles = tiles_per_batch * batch
    assert nt // tm == n_lat_tiles + 1 and batch * lc == tm
    row = lambda w: pl.BlockSpec((tm, w), lambda i: (i, 0))
    lat = pl.BlockSpec((tm, VW), lambda i: (jnp.where(i < n_lat_tiles, i % tiles_per_batch, 0),
                                            jnp.where(i < n_lat_tiles, i // tiles_per_batch, 0)))
    ctx_specs, ctx_args = [], []
    for b in range(batch):
        for arr in (o_f, o_b):
            ctx_specs.append(pl.BlockSpec((lc, VW), functools.partial(lambda i, b: (l // lc, b), b=b)))
            ctx_args.append(arr)
    return pl.pallas_call(
        functools.partial(_out_kernel, batch=batch, n_lat_tiles=n_lat_tiles),
        grid=(nt // tm,),
        in_specs=[
            row(D_MODEL), _mod_spec(tiles_per_batch, batch),
            row(VW), row(VW), row(VW), row(D_MODEL), row(D_MODEL),
            _const_spec((1, DV)),
            _const_spec((VW, D_MODEL)), _const_spec((VW, D_MODEL)), _const_spec((D_MODEL, D_MODEL)),
            lat, lat,
        ] + ctx_specs,
        out_specs=row(D_MODEL),
        out_shape=jax.ShapeDtypeStruct((nt, D_MODEL), F32),
        compiler_params=_params(("parallel",)),
        name="mixer_out",
    )(xs, mods, ret, srg, sdz, sga, sgb, dn_norm_w.reshape(1, DV).astype(F32), w_ret, w_dn, w_o, o_f, o_b,
      *ctx_args)


def _rope_tables(l, tm):
    half = DK // 2
    n_freq = half // 2
    inv = ROPE_BASE ** (-jnp.arange(n_freq, dtype=F32) / n_freq)
    rows = l // GRID_W
    ang_r = jnp.arange(rows, dtype=F32)[:, None] * inv
    ang_c = jnp.arange(GRID_W, dtype=F32)[:, None] * inv
    ang = jnp.concatenate([
        jnp.broadcast_to(ang_r[:, None, :], (rows, GRID_W, n_freq)),
        jnp.broadcast_to(ang_c[None, :, :], (rows, GRID_W, n_freq)),
    ], axis=-1).reshape(l, half)
    cos, sin = jnp.cos(ang), jnp.sin(ang)
    cos_t = jnp.concatenate([jnp.concatenate([cos, cos], axis=1), jnp.ones((tm, DK), F32)], axis=0)
    sin_t = jnp.concatenate([jnp.concatenate([-sin, sin], axis=1), jnp.zeros((tm, DK), F32)], axis=0)
    return cos_t, sin_t


def kernel(x, c, ctx, c_ctx, ada_w, ada_b, norm_w, ffn1_wgu, ffn1_wd, w_in, dn_conv_w, dn_a_log, dn_dt_bias,
           dn_norm_w, w_ret_out, w_dn_out, w_o, ffn2_wgu, ffn2_wd, final_norm_w):
    batch, l, _ = x.shape
    lc = ctx.shape[1]
    depth = ada_w.shape[0]
    assert batch + 1 <= SUBLANES and l % DN_CHUNK == 0 and lc % DN_CHUNK == 0 and l % GRID_W == 0
    tm = _row_tile(l, batch * lc)
    tiles_per_batch = l // tm

    assert batch * lc == tm
    xs = x.reshape(batch * l, D_MODEL)
    ctx_rows = ctx.reshape(batch * lc, D_MODEL)
    cmat =jnp.zeros((SUBLANES, D_MODEL), F32).at[:batch].set(c).at[batch].set(c_ctx)
    mods_all = _ada_all(cmat, ada_w, ada_b).reshape(depth, SUBLANES, N_MOD, D_MODEL)
    cos_t, sin_t = _rope_tables(l, tm)
    log_gamma = jnp.log1p(-jnp.power(2.0, -5.0 - jnp.arange(HEADS, dtype=F32)))
    log_gamma_t = jnp.broadcast_to(log_gamma[:, None, None], (HEADS, 1, LANES))

    for layer in range(depth):
        mods = mods_all[layer, :batch + 1]
        w_l = w_in[layer]
        w_main = jnp.concatenate([w_l[:, :_OFF_GA], w_l[:, _OFF_GA + N_BA:]], axis=1).astype(BF16)
        w_ba_cols = w_l[:, _OFF_GA:_OFF_GA + N_BA]
        w_ba = jnp.pad(w_ba_cols, ((0, 0), (0, LANES - N_BA))).astype(BF16)
        w_bat = w_ba_cols.T.astype(BF16)

        xs = _ffn(xs, mods, norm_w[layer, 0], ffn1_wgu[layer].astype(BF16), ffn1_wd[layer].astype(BF16),
                  final_norm_w, 0, tm, tiles_per_batch, batch, ctx_rows=ctx_rows if layer == 0 else None)
        rq, rk, rv, srg, dpre, sdz, ba, bat, sga, sgb = _inproj(
            xs, mods, norm_w[layer, 1], w_main, w_ba, w_bat, cos_t, sin_t, tm, tiles_per_batch, batch)
        ret = _retention(rq, rk, rv, log_gamma_t, batch, l, lc)
        p1, p2, u, eg = _dnprep(dpre, ba, bat, dn_conv_w[layer], dn_a_log[layer], dn_dt_bias[layer], batch, l, lc)
        o_f, o_b = _dnseq(p1, p2, u, eg, batch, l, lc)
        xs = _mixer_out(xs, mods, ret, srg, sdz, sga, sgb, dn_norm_w[layer], w_ret_out[layer].astype(BF16),
                        w_dn_out[layer].astype(BF16), w_o[layer].astype(BF16), o_f, o_b,
                        tm, tiles_per_batch, batch, l, lc)
        xs = _ffn(xs, mods, norm_w[layer, 2], ffn2_wgu[layer].astype(BF16), ffn2_wd[layer].astype(BF16),
                  final_norm_w, 6, tm, tiles_per_batch, batch, final=layer == depth - 1)

    return xs.reshape(batch, l, D_MODEL)
```

```python
import functools
import math

import jax
import jax.numpy as jnp
from jax import lax
from jax.experimental import pallas as pl
from jax.experimental.pallas import tpu as pltpu

F32 = jnp.float32
BF16 = jnp.bfloat16

D_MODEL = 1024
FFN_HIDDEN = 2816
N_MOD = 9
GRID_W = 64
ROPE_BASE = 10000.0
EPS = 1e-6

HEADS = 4
DK = 128
DV = 256
QK = HEADS * DK
VW = HEADS * DV
DN_CONV = 5
CONV_CH = 2 * QK + VW
N_BA = 4 * HEADS

LANES = 128
SUBLANES = 8
MXU_N = 256
VMEM_LIMIT = 56 * 1024 * 1024

DN_CHUNK = 128
DN_PREP_CHUNKS = 2
FFN_CHUNK = MXU_N


def _dot(a, b):
    return jnp.dot(a, b, preferred_element_type=F32)


def _dot_nt(a, b):
    return lax.dot_general(a, b, (((1,), (1,)), ((), ())), preferred_element_type=F32)


def _silu(x):
    return x * jax.nn.sigmoid(x)


def _row_tile(l, lc_total):
    for t in (512, 256, 128):
        if l % t == 0 and lc_total % t == 0:
            return t
    raise ValueError("sequence lengths must be multiples of 128")


def _params(sem):
    return pltpu.CompilerParams(dimension_semantics=sem, vmem_limit_bytes=VMEM_LIMIT)


def _const_spec(shape):
    nd = len(shape)
    return pl.BlockSpec(shape, lambda *_: (0,) * nd, pipeline_mode=pl.Buffered(1))


def _layer_spec(shape, layer):
    return pl.BlockSpec((None,) + tuple(shape), lambda *_: (layer, 0, 0), pipeline_mode=pl.Buffered(1))


def _ada_kernel(c_ref, w_ref, b_ref, o_ref):
    c = c_ref[...]
    s = _silu(c)
    o_ref[0] = jnp.dot(s, w_ref[0], preferred_element_type=F32,
                       precision=lax.Precision.HIGHEST) + b_ref[0]


def _ada_all(cmat, ada_w, ada_b):
    depth = ada_w.shape[0]
    n = ada_w.shape[2]
    tn = 1024
    return pl.pallas_call(
        _ada_kernel,
        grid=(depth, n // tn),
        in_specs=[
            pl.BlockSpec((SUBLANES, D_MODEL), lambda l, j: (0, 0)),
            pl.BlockSpec((1, D_MODEL, tn), lambda l, j: (l, 0, j)),
            pl.BlockSpec((1, 1, tn), lambda l, j: (l, 0, j)),
        ],
        out_specs=pl.BlockSpec((1, SUBLANES, tn), lambda l, j: (l, 0, j)),
        out_shape=jax.ShapeDtypeStruct((depth, SUBLANES, n), F32),
        compiler_params=_params(("parallel", "parallel")),
        name="ada",
    )(cmat, ada_w, ada_b.reshape(depth, 1, n))


def _norm_mod(x, nw, shift, scale):
    ms = jnp.mean(x * x, axis=-1, keepdims=True)
    h = x * lax.rsqrt(ms + EPS) * nw
    return h * (1.0 + scale) + shift


def _mod_spec(tiles_per_batch, batch):
    return pl.BlockSpec((1, N_MOD, D_MODEL), lambda i: (jnp.minimum(i // tiles_per_batch, batch), 0, 0))


def _ffn_kernel(x_ref, c_ref, m_ref, nw_ref, wgu_ref, wd_ref, fw_ref, o_ref, *, k0, final, n_lat_tiles):
    x = x_ref[...] if n_lat_tiles is None else jnp.where(pl.program_id(0) < n_lat_tiles, x_ref[...], c_ref[...])
    shift = m_ref[0, k0:k0 + 1, :]
    scale = m_ref[0, k0 + 1:k0 + 2, :]
    gate = m_ref[0, k0 + 2:k0 + 3, :]
    h = _norm_mod(x, nw_ref[...], shift, scale).astype(BF16)
    acc = jnp.zeros(x.shape, F32)
    for c in range(FFN_HIDDEN // FFN_CHUNK):
        lo = c * FFN_CHUNK
        g = _dot(h, wgu_ref[:, lo:lo + FFN_CHUNK])
        u = _dot(h, wgu_ref[:, FFN_HIDDEN + lo:FFN_HIDDEN + lo + FFN_CHUNK])
        a = (_silu(g) * u).astype(BF16)
        acc = acc + _dot(a, wd_ref[lo:lo + FFN_CHUNK, :])
    out = x + (0.5 * gate) * acc
    if final:
        out = out * lax.rsqrt(jnp.mean(out * out, axis=-1, keepdims=True) + EPS) * fw_ref[...]
    o_ref[...] = out


def _ffn(xs, mods, nw, wgu, wd, layer, final_w, k0, tm, tiles_per_batch, batch, final=False, ctx_rows=None):
    n_lat_tiles = tiles_per_batch * batch
    if ctx_rows is None:
        rows = n_lat_tiles * tm if final else xs.shape[0]
        x_spec = pl.BlockSpec((tm, D_MODEL), lambda i: (i, 0))
        c_arg, c_spec = final_w.reshape(1, D_MODEL), _const_spec((1, D_MODEL))
    else:
        assert ctx_rows.shape[0] == tm and not final
        rows = (n_lat_tiles + 1) * tm
        x_spec = pl.BlockSpec((tm, D_MODEL), lambda i: (jnp.minimum(i, n_lat_tiles - 1), 0))
        c_arg, c_spec = ctx_rows, _const_spec((tm, D_MODEL))
    return pl.pallas_call(
        functools.partial(_ffn_kernel, k0=k0, final=final, n_lat_tiles=None if ctx_rows is None else n_lat_tiles),
        grid=(rows // tm,),
        in_specs=[
            x_spec,
            c_spec,
            _mod_spec(tiles_per_batch, batch),
            _const_spec((1, D_MODEL)),
            _layer_spec((D_MODEL, 2 * FFN_HIDDEN), layer),
            _layer_spec((FFN_HIDDEN, D_MODEL), layer),
            _const_spec((1, D_MODEL)),
        ],
        out_specs=pl.BlockSpec((tm, D_MODEL), lambda i: (i, 0)),
        out_shape=jax.ShapeDtypeStruct((rows, D_MODEL), F32),
        compiler_params=_params(("parallel",)),
        name="ffn",
    )(xs, c_arg, mods, nw.reshape(1, D_MODEL), wgu, wd, final_w.reshape(1, D_MODEL))


_OFF_RQ, _OFF_RK, _OFF_RV, _OFF_RG = 0, QK, 2 * QK, 2 * QK + VW
_OFF_DQKV = 2 * QK + 2 * VW
_OFF_DZ = _OFF_DQKV + CONV_CH
_OFF_GA = _OFF_DZ + VW


def _inproj_kernel(x_ref, m_ref, x0_ref, m0_ref, nw_ref, w_ref, wg_ref, wba_ref, wbat_ref, cos_ref, sin_ref,
                   rq_ref, rk_ref, rv_ref, srg_ref, dpre_ref, sdz_ref, ba_ref, bat_ref, sga_ref, sgb_ref, h_ref):
    i = pl.program_id(0)

    @pl.when(i == 0)
    def _():
        h_ref[0] = _norm_mod(x0_ref[...], nw_ref[...], m0_ref[0, 3:4, :], m0_ref[0, 4:5, :]).astype(BF16)

    slot = i % 2
    h = h_ref[slot]
    h_ref[1 - slot] = _norm_mod(x_ref[...], nw_ref[...], m_ref[0, 3:4, :], m_ref[0, 4:5, :]).astype(BF16)
    cos = cos_ref[...]
    sin = sin_ref[...]
    blk = 2 * DK

    def rope_job(o_ref, off, scale, j):
        def run():
            p = _dot(h, w_ref[:, off + j * blk:off + (j + 1) * blk]) * scale
            outs = []
            for hh in range(blk // DK):
                slab = p[:, hh * DK:(hh + 1) * DK]
                outs.append(slab * cos + pltpu.roll(slab, DK // 2, 1) * sin)
            o_ref[:, j * blk:(j + 1) * blk] = jnp.concatenate(outs, axis=1).astype(BF16)
        return run

    def act_job(o_ref, off, act, j, wsrc=w_ref):
        def run():
            o_ref[:, j * blk:(j + 1) * blk] = act(_dot(h, wsrc[:, off + j * blk:off + (j + 1) * blk])).astype(BF16)
        return run

    def pre_job(j):
        def run():
            dpre_ref[:, j * blk:(j + 1) * blk] = _dot(h, w_ref[:, _OFF_DQKV + j * blk:_OFF_DQKV + (j + 1) * blk])
        return run

    jobs = [rope_job(rq_ref, _OFF_RQ, 1.0, j) for j in range(QK // blk)]
    jobs += [rope_job(rk_ref, _OFF_RK, DK ** -0.5, j) for j in range(QK // blk)]
    jobs += [act_job(rv_ref, _OFF_RV, lambda p: p, j) for j in range(VW // blk)]
    jobs += [act_job(srg_ref, _OFF_RG, _silu, j) for j in range(VW // blk)]
    jobs += [act_job(sdz_ref, _OFF_DZ, _silu, j) for j in range(VW // blk)]
    jobs += [act_job(sga_ref, 0, jax.nn.sigmoid, j, wg_ref) for j in range(D_MODEL // blk)]
    jobs += [act_job(sgb_ref, D_MODEL, jax.nn.sigmoid, j, wg_ref) for j in range(D_MODEL // blk)]
    jobs += [pre_job(j) for j in range(CONV_CH // blk)]
    for job in jobs:
        job()
    ba_ref[...] = _dot(h, wba_ref[...])
    bat_ref[...] = _dot_nt(wbat_ref[...], h)


def _inproj(xs, mods, nw, w_main, w_gates, w_ba, w_bat, layer, cos_t, sin_t, tm, tiles_per_batch, batch):
    nt = xs.shape[0]
    n_lat_tiles = tiles_per_batch * batch
    last = nt // tm - 1

    def rope_map(i):
        return (jnp.where(i < n_lat_tiles, i % tiles_per_batch, tiles_per_batch), 0)

    row = lambda w: pl.BlockSpec((tm, w), lambda i: (i, 0))
    outs = [
        (QK, BF16), (QK, BF16), (VW, BF16), (VW, BF16), (CONV_CH, F32), (VW, BF16),
        (LANES, F32), None, (D_MODEL, BF16), (D_MODEL, BF16),
    ]
    out_specs, out_shape = [], []
    for o in outs:
        if o is None:
            out_specs.append(pl.BlockSpec((N_BA, tm), lambda i: (0, i)))
            out_shape.append(jax.ShapeDtypeStruct((N_BA, nt), F32))
        else:
            out_specs.append(row(o[0]))
            out_shape.append(jax.ShapeDtypeStruct((nt, o[0]), o[1]))
    return pl.pallas_call(
        _inproj_kernel,
        grid=(nt // tm,),
        in_specs=[
            pl.BlockSpec((tm, D_MODEL), lambda i: (jnp.minimum(i + 1, last), 0)),
            pl.BlockSpec((1, N_MOD, D_MODEL),
                         lambda i: (jnp.minimum(jnp.minimum(i + 1, last) // tiles_per_batch, batch), 0, 0)),
            _const_spec((tm, D_MODEL)),
            _const_spec((1, N_MOD, D_MODEL)),
            _const_spec((1, D_MODEL)),
            _layer_spec((D_MODEL, _OFF_GA), layer),
            _layer_spec((D_MODEL, 2 * D_MODEL), layer),
            _layer_spec((D_MODEL, LANES), layer),
            _layer_spec((N_BA, D_MODEL), layer),
            pl.BlockSpec((tm, DK), rope_map),
            pl.BlockSpec((tm, DK), rope_map),
        ],
        out_specs=out_specs,
        out_shape=out_shape,
        scratch_shapes=[pltpu.VMEM((2, tm, D_MODEL), BF16)],
        compiler_params=_params(("arbitrary",)),
        name="inproj",
    )(xs, mods, xs, mods, nw.reshape(1, D_MODEL), w_main, w_gates, w_ba, w_bat, cos_t, sin_t)


def _ret_kernel(lg_ref, q_ref, k_ref, v_ref, o_ref,
                sf_ref, sb_ref, sball_ref, dm_ref, eq_ref, ekf_ref, ekb_ref, *, nc, n, c):
    ph = pl.program_id(1)
    t = pl.program_id(2)

    @pl.when((ph == 0) & (t == 0))
    def _():
        sb_ref[...] = jnp.zeros_like(sb_ref)
        ii = lax.broadcasted_iota(jnp.int32, (c, 1), 0).astype(F32)
        jj = lax.broadcasted_iota(jnp.int32, (1, c), 1).astype(F32)
        for h in range(HEADS):
            lg = lg_ref[h]
            dm_ref[h] = jnp.exp(lg[:, :1] * jnp.abs(ii - jj))
            eq_ref[h, :, :DK] = jnp.exp(lg * (ii + 1.0))
            eq_ref[h, :, DK:] = jnp.exp(lg * (float(c) - ii))
            ekf_ref[h] = jnp.exp(lg * (float(c) - 1.0 - ii))
            ekb_ref[h] = jnp.exp(lg * ii)

    def kv_update(h, s_ref, ek_ref):
        k = k_ref[:, h * DK:(h + 1) * DK].astype(F32) * ek_ref[h]
        kv = _dot(k.T.astype(BF16), v_ref[:, h * DV:(h + 1) * DV])
        gamma_c = jnp.exp(lg_ref[h][:, :1] * float(c))
        s_ref[h] = s_ref[h] * gamma_c + kv

    @pl.when(ph == 0)
    def _():
        pos = jnp.where(t < nc, nc - 1 - t, nc + n - 1 - (t - nc))
        for h in range(HEADS):
            sball_ref[pos, h] = sb_ref[h].astype(BF16)
            kv_update(h, sb_ref, ekb_ref)

    @pl.when(ph == 1)
    def _():
        @pl.when(t == 0)
        def _():
            sf_ref[...] = jnp.zeros_like(sf_ref)
        hs = range(HEADS)
        qs = [q_ref[:, h * DK:(h + 1) * DK] for h in hs]
        ss = [(_dot_nt(qs[h], k_ref[:, h * DK:(h + 1) * DK]) * dm_ref[h]).astype(BF16) for h in hs]
        q2s = [(jnp.concatenate([qs[h].astype(F32)] * 2, axis=1) * eq_ref[h]).astype(BF16) for h in hs]
        s2s = [jnp.concatenate([sf_ref[h].astype(BF16), sball_ref[t, h]], axis=0) for h in hs]
        os_ = [_dot(ss[h], v_ref[:, h * DV:(h + 1) * DV]) + _dot(q2s[h], s2s[h]) for h in hs]
        for h in hs:
            o = os_[h]
            o = o * lax.rsqrt(jnp.mean(o * o, axis=-1, keepdims=True) + EPS)
            o_ref[:, h * DV:(h + 1) * DV] = o.astype(BF16)
        for h in hs:
            kv_update(h, sf_ref, ekf_ref)


def _retention(rq, rk, rv, log_gamma_t, batch, l, lc):
    nt = rq.shape[0]
    c = min(256, lc)
    nc, n = lc // c, l // c
    lat0 = lambda b: b * n
    ctx0 = lambda b: batch * n + b * nc

    def chunk(b, ph, t):
        fwd = jnp.where(t < nc, ctx0(b) + t, lat0(b) + t - nc)
        bwd = jnp.where(t < nc, ctx0(b) + nc - 1 - t, lat0(b) + n - 1 - (t - nc))
        return jnp.where(ph == 0, bwd, fwd)

    def q_chunk(b, ph, t):
        return jnp.where(ph == 0, ctx0(b), chunk(b, ph, t))

    return pl.pallas_call(
        functools.partial(_ret_kernel, nc=nc, n=n, c=c),
        grid=(batch, 2, nc + n),
        in_specs=[
            _const_spec((HEADS, 1, LANES)),
            pl.BlockSpec((c, QK), lambda b, ph, t: (q_chunk(b, ph, t), 0)),
            pl.BlockSpec((c, QK), lambda b, ph, t: (chunk(b, ph, t), 0)),
            pl.BlockSpec((c, VW), lambda b, ph, t: (chunk(b, ph, t), 0)),
        ],
        out_specs=pl.BlockSpec((c, VW), lambda b, ph, t: (q_chunk(b, ph, t), 0)),
        out_shape=jax.ShapeDtypeStruct((nt, VW), BF16),
        scratch_shapes=[
            pltpu.VMEM((HEADS, DK, DV), F32),
            pltpu.VMEM((HEADS, DK, DV), F32),
            pltpu.VMEM((nc + n, HEADS, DK, DV), BF16),
            pltpu.VMEM((HEADS, c, c), F32),
            pltpu.VMEM((HEADS, c, 2 * DK), F32),
            pltpu.VMEM((HEADS, c, DK), F32),
            pltpu.VMEM((HEADS, c, DK), F32),
        ],
        compiler_params=_params(("parallel", "arbitrary", "arbitrary")),
        name="retention",
    )(log_gamma_t, rq, rk, rv)


TRI_BASE = 8


def _tri_masks(ri, ci, c):
    as16 = lambda m: jnp.where(m, 1.0, 0.0).astype(BF16)
    diag = as16((ri // TRI_BASE) == (ci // TRI_BASE))
    offs = []
    s = TRI_BASE
    while s < c:
        offs.append(as16(((ri // s) ^ (ci // s)) == 1))
        s *= 2
    return diag, offs


def _unit_tri_inverses(mats, eye, masks):
    diag, offs = masks
    a16 = [a.astype(BF16) for a in mats]
    qs = [a * diag for a in a16]
    ts = [eye - q.astype(F32) for q in qs]
    ts16 = [t.astype(BF16) for t in ts]
    for _ in range(int(math.log2(TRI_BASE)) - 1):
        qs = [_dot(q, q).astype(BF16) for q in qs]
        ts = [t + _dot(t16, q) for t, t16, q in zip(ts, ts16, qs)]
        ts16 = [t.astype(BF16) for t in ts]
    for off in offs:
        half = [_dot(t16, a * off).astype(BF16) for t16, a in zip(ts16, a16)]
        ts = [t - _dot(hf, t16) for t, hf, t16 in zip(ts, half, ts16)]
        ts16 = [t.astype(BF16) for t in ts]
    return ts16


def _softplus(x):
    return jnp.maximum(x, 0.0) + jnp.log1p(jnp.exp(-jnp.abs(x)))


def _split3(x):
    x1 = x.astype(BF16)
    r = x - x1.astype(F32)
    x2 = r.astype(BF16)
    x3 = (r - x2.astype(F32)).astype(BF16)
    return x1, x2, x3


def _ones_left(ones, x):
    p1, p2, p3 = _split3(x)
    return _dot(ones, p1) + _dot(ones, p2) + _dot(ones, p3)


def _ones_right(x, ones):
    p1, p2, p3 = _split3(x)
    return _dot(p1, ones) + _dot(p2, ones) + _dot(p3, ones)


def _dnprep_kernel(main_ref, prev_ref, next_ref, ba_ref, bat_ref, cw_ref, arow_ref, acol_ref, dtrow_ref, dtcol_ref,
                   p1_ref, p2_ref, u_ref, eg_ref, ext_ref, nat_ref, *, nb, lat_chunks, ctx_chunks, n_lat):
    c = DN_CHUNK
    rows = nb * c
    g = pl.program_id(0)

    n_slab = CONV_CH // LANES
    for sl in range(n_slab):
        cols = slice(sl * LANES, (sl + 1) * LANES)
        ext_ref[sl, 0:SUBLANES, :] = prev_ref[:, cols]
        ext_ref[sl, SUBLANES:SUBLANES + rows, :] = main_ref[:, cols]
        ext_ref[sl, SUBLANES + rows:2 * SUBLANES + rows, :] = next_ref[:, cols]
    phases = c // SUBLANES
    pad = DN_CONV // 2
    sub = lax.broadcasted_iota(jnp.int32, (SUBLANES, CONV_CH), 0)
    zcache = {}

    def phase_rows(n, t):
        if (n, t) not in zcache:
            v = jnp.concatenate(
                [ext_ref[sl, pl.ds(SUBLANES + n * c + t, SUBLANES, stride=phases), :] for sl in range(n_slab)], axis=1)
            gc = g * nb + n
            pos = jnp.where(gc < n_lat, gc % lat_chunks, (gc - n_lat) % ctx_chunks)
            if t < 0:
                v = jnp.where((sub == 0) & (pos == 0), 0.0, v)
            elif t >= phases:
                last = jnp.where(gc < n_lat, lat_chunks, ctx_chunks) - 1
                v = jnp.where((sub == SUBLANES - 1) & (pos == last), 0.0, v)
            zcache[n, t] = v
        return zcache[n, t]

    def conv_phase(n, j):
        acc = phase_rows(n, j) * cw_ref[pad]
        for k in range(DN_CONV):
            if k != pad:
                acc = acc + phase_rows(n, j + k - pad) * cw_ref[k]
        yj = _silu(acc)
        parts = []
        for hh in range(2 * HEADS):
            s = yj[:, hh * DK:(hh + 1) * DK]
            scale = (DK ** -0.5) if hh < HEADS else 1.0
            parts.append(s * (lax.rsqrt(jnp.sum(s * s, axis=-1, keepdims=True) + EPS) * scale))
        for sl in range(2 * HEADS, n_slab):
            parts.append(yj[:, sl * LANES:(sl + 1) * LANES])
        for sl in range(n_slab):
            nat_ref[sl, n * c + j * SUBLANES:n * c + (j + 1) * SUBLANES, :] = parts[sl]

    def natural(sl, n):
        return jnp.concatenate(
            [nat_ref[sl, pl.ds(n * c + (c // 2) * (r % 2) + r // 2, SUBLANES, stride=SUBLANES), :]
             for r in range(phases)], axis=0)

    ri = lax.broadcasted_iota(jnp.int32, (c, c), 0)
    ci = lax.broadcasted_iota(jnp.int32, (c, c), 1)
    eye = (ri == ci).astype(F32)
    masks = _tri_masks(ri, ci, c)
    lower16 = jnp.where(ri >= ci, 1.0, 0.0).astype(BF16)
    upper16 = jnp.where(ri <= ci, 1.0, 0.0).astype(BF16)
    tri_col = jnp.concatenate([lower16, upper16], axis=0)
    tri_row = jnp.concatenate([upper16, lower16], axis=1)
    qs, ks, vs, kts, qks = {}, {}, {}, {}, {}

    def build_items(n):
        items = []
        tok = slice(n * c, (n + 1) * c)
        ba = ba_ref[tok, :]
        sig = jax.nn.sigmoid(ba)
        g_col = -jnp.exp(arow_ref[...]) * _softplus(ba + dtrow_ref[...])
        cs = _ones_left(tri_col, g_col)
        gc_col = (cs[:c], cs[c:])
        g_row = -jnp.exp(acol_ref[...]) * _softplus(bat_ref[:, tok] + dtcol_ref[...])
        rs = _ones_right(g_row, tri_row)
        gc_row = (rs[:, :c], rs[:, c:])
        for h in range(HEADS):
            q = natural(h, n)
            k = natural(HEADS + h, n)
            kb16 = k.astype(BF16)
            qs[n, h] = q
            ks[n, h] = k
            vs[n, h] = jnp.concatenate([natural(2 * HEADS + 2 * h, n), natural(2 * HEADS + 2 * h + 1, n)], axis=1)
            kts[n, h] = k.T
            qks[n, h] = _dot_nt(q.astype(BF16), kb16)
            kk = _dot_nt(kb16, kb16)
            for d in range(2):
                lane = 2 * HEADS + d * HEADS + h
                gcol = gc_col[d][:, lane:lane + 1]
                grow = gc_row[d][lane:lane + 1, :]
                incl = (ri >= ci) if d == 0 else (ri <= ci)
                strict = (ri > ci) if d == 0 else (ri < ci)
                decay = jnp.where(incl, jnp.exp(jnp.where(incl, gcol - grow, 0.0)), 0.0)
                beta = sig[:, d * HEADS + h:d * HEADS + h + 1]
                a = jnp.where(strict, beta * kk * decay, 0.0)
                items.append((n, h, d, gcol, grow, decay, beta, a))
        return items

    def emit(item, tmat):
        n, h, d, gcol, grow, decay, beta, _ = item
        hd = d * HEADS + h
        q, k, v = qs[n, h], ks[n, h], vs[n, h]
        egc = jnp.exp(gcol)
        rhs = jnp.concatenate([v * beta, k * (beta * egc)], axis=1).astype(BF16)
        uw = _dot(tmat.astype(BF16), rhs)
        glast = grow[:, c - 1:c] if d == 0 else grow[:, 0:1]
        u_ref[n, hd] = uw[:, :DV].astype(BF16)
        p1_ref[n, hd, 0:c, :] = uw[:, DV:].astype(BF16)
        p1_ref[n, hd, c:2 * c, :] = (q * egc).astype(BF16)
        p2_ref[n, hd, 0:c, :] = (qks[n, h] * decay).astype(BF16)
        p2_ref[n, hd, c:2 * c, :] = (kts[n, h] * jnp.exp(glast - grow)).astype(BF16)
        eg_ref[n, hd] = jnp.broadcast_to(jnp.exp(glast), (1, DV))

    for n in range(nb):
        for j in range(phases):
            conv_phase(n, j)
    items = [it for n in range(nb) for it in build_items(n)]
    for it, tmat in zip(items, _unit_tri_inverses([it[7] for it in items], eye, masks)):
        emit(it, tmat)


def _dnprep(dpre, ba, bat, conv_w, a_log, dt_bias, batch, l, lc):
    nt = dpre.shape[0]
    c = DN_CHUNK
    nch = nt // c
    nblk8 = nt // SUBLANES
    flat_a = jnp.concatenate([jnp.zeros((2 * HEADS,), F32), a_log.reshape(-1).astype(F32)])
    flat_dt = jnp.concatenate([jnp.zeros((2 * HEADS,), F32), dt_bias.reshape(-1).astype(F32)])
    arow = jnp.pad(flat_a, (0, LANES - N_BA)).reshape(1, LANES)
    dtrow = jnp.pad(flat_dt, (0, LANES - N_BA)).reshape(1, LANES)
    acol = flat_a.reshape(N_BA, 1)
    dtcol = flat_dt.reshape(N_BA, 1)
    cw8 = jnp.broadcast_to(conv_w.astype(F32)[:, None, :], (DN_CONV, SUBLANES, CONV_CH))
    hd = 2 * HEADS
    nb = math.gcd(DN_PREP_CHUNKS, l // c, batch * lc // c)
    rows = nb * c
    per = rows // SUBLANES
    return pl.pallas_call(
        functools.partial(_dnprep_kernel, nb=nb, lat_chunks=l // c, ctx_chunks=lc // c, n_lat=batch * l // c),
        grid=(nch // nb,),
        in_specs=[
            pl.BlockSpec((rows, CONV_CH), lambda g: (g, 0)),
            pl.BlockSpec((SUBLANES, CONV_CH), lambda g: (jnp.maximum(g * per - 1, 0), 0)),
            pl.BlockSpec((SUBLANES, CONV_CH), lambda g: (jnp.minimum((g + 1) * per, nblk8 - 1), 0)),
            pl.BlockSpec((rows, LANES), lambda g: (g, 0)),
            pl.BlockSpec((N_BA, rows), lambda g: (0, g)),
            _const_spec((DN_CONV, SUBLANES, CONV_CH)),
            _const_spec((1, LANES)),
            _const_spec((N_BA, 1)),
            _const_spec((1, LANES)),
            _const_spec((N_BA, 1)),
        ],
        out_specs=[
            pl.BlockSpec((nb, hd, 2 * c, DK), lambda g: (g, 0, 0, 0)),
            pl.BlockSpec((nb, hd, 2 * c, c), lambda g: (g, 0, 0, 0)),
            pl.BlockSpec((nb, hd, c, DV), lambda g: (g, 0, 0, 0)),
            pl.BlockSpec((nb, hd, 1, DV), lambda g: (g, 0, 0, 0)),
        ],
        out_shape=[
            jax.ShapeDtypeStruct((nch, hd, 2 * c, DK), BF16),
            jax.ShapeDtypeStruct((nch, hd, 2 * c, c), BF16),
            jax.ShapeDtypeStruct((nch, hd, c, DV), BF16),
            jax.ShapeDtypeStruct((nch, hd, 1, DV), F32),
        ],
        scratch_shapes=[pltpu.VMEM((CONV_CH // LANES, rows + 2 * SUBLANES, LANES), F32),
                        pltpu.VMEM((CONV_CH // LANES, rows, LANES), F32)],
        compiler_params=_params(("parallel",)),
        name="dnprep",
    )(dpre, dpre, dpre, ba, bat, cw8, arow, acol, dtrow, dtcol)


def _dnseq_kernel(*refs, batch):
    c = DN_CHUNK
    ins = refs[:8 * batch]
    of_ref, ob_ref, s_ref = refs[8 * batch:]

    @pl.when(pl.program_id(0) == 0)
    def _():
        s_ref[...] = jnp.zeros_like(s_ref)

    chains = []
    for b in range(batch):
        for d in range(2):
            p1_ref, p2_ref, u_ref, eg_ref = ins[8 * b + 4 * d:8 * b + 4 * d + 4]
            for h in range(HEADS):
                chains.append((b, d, h, p1_ref, p2_ref, u_ref, eg_ref))
    sidx = lambda b, d, h: (b * 2 + d) * HEADS + h
    rs = [_dot(p1_ref[0, h], s_ref[sidx(b, d, h)].astype(BF16))
          for (b, d, h, p1_ref, _, _, _) in chains]
    vnews = [(u_ref[0, h].astype(F32) - r[:c]).astype(BF16)
             for (b, d, h, _, _, u_ref, _), r in zip(chains, rs)]
    r2s = [_dot(p2_ref[0, h], vnew)
           for (b, d, h, _, p2_ref, _, _), vnew in zip(chains, vnews)]
    for (b, d, h, _, _, _, eg_ref), r, r2 in zip(chains, rs, r2s):
        o_ref = of_ref if d == 0 else ob_ref
        lo = b * VW + h * DV
        o_ref[:, lo:lo + DV] = (r[c:] + r2[:c]).astype(BF16)
        s_ref[sidx(b, d, h)] = s_ref[sidx(b, d, h)] * eg_ref[0, h] + r2[c:]


def _dnseq(p1, p2, u, eg, batch, l, lc):
    c = DN_CHUNK
    nc, n = lc // c, l // c

    def fwd(b, t):
        return jnp.where(t < nc, batch * n + b * nc + t, b * n + t - nc)

    def bwd(b, t):
        return jnp.where(t < nc, batch * n + b * nc + nc - 1 - t, b * n + n - 1 - (t - nc))

    in_specs, args = [], []
    for b in range(batch):
        for d, cm in enumerate((fwd, bwd)):
            for arr in (p1, p2, u, eg):
                shp = (1, HEADS) + arr.shape[2:]
                in_specs.append(pl.BlockSpec(
                    shp, functools.partial(lambda t, cm, b, d: (cm(b, t), d, 0, 0), cm=cm, b=b, d=d)))
                args.append(arr)
    return pl.pallas_call(
        functools.partial(_dnseq_kernel, batch=batch),
        grid=(nc + n,),
        in_specs=in_specs,
        out_specs=[
            pl.BlockSpec((c, batch * VW), lambda t: (jnp.where(t < nc, n + t, t - nc), 0)),
            pl.BlockSpec((c, batch * VW), lambda t: (jnp.where(t < nc, n + nc - 1 - t, n - 1 - (t - nc)), 0)),
        ],
        out_shape=[jax.ShapeDtypeStruct((l + lc, batch * VW), BF16)] * 2,
        scratch_shapes=[pltpu.VMEM((batch * 2 * HEADS, DK, DV), F32)],
        compiler_params=_params(("arbitrary",)),
        name="dnseq",
    )(*args)


def _out_kernel(*refs, batch, n_lat_tiles):
    (x_ref, m_ref, ret_ref, srg_ref, sdz_ref, sga_ref, sgb_ref, dnw_ref, wr_ref, wdn_ref, wo_ref,
     of_ref, ob_ref) = refs[:13]
    ctx_refs = refs[13:13 + 2 * batch]
    out_ref = refs[13 + 2 * batch]
    x = x_ref[...]
    gate = m_ref[0, 5:6, :]
    ret = (ret_ref[...].astype(F32) * srg_ref[...].astype(F32)).astype(BF16)
    yr = _dot(ret, wr_ref[...])
    dn_lat = of_ref[...].astype(F32) + ob_ref[...].astype(F32)
    dn_ctx = jnp.concatenate([ctx_refs[2 * b][...].astype(F32) + ctx_refs[2 * b + 1][...].astype(F32)
                              for b in range(batch)], axis=0)
    dn_o = jnp.where(pl.program_id(0) < n_lat_tiles, dn_lat, dn_ctx)
    dnw = dnw_ref[...]
    slabs = []
    for h in range(HEADS):
        o = dn_o[:, h * DV:(h + 1) * DV]
        slabs.append(o * lax.rsqrt(jnp.mean(o * o, axis=-1, keepdims=True) + EPS) * dnw)
    dn = (jnp.concatenate(slabs, axis=1) * sdz_ref[...].astype(F32)).astype(BF16)
    yd = _dot(dn, wdn_ref[...])
    y = sga_ref[...].astype(F32) * yr + sgb_ref[...].astype(F32) * yd
    out_ref[...] = x + gate * _dot(y.astype(BF16), wo_ref[...])


def _mixer_out(xs, mods, ret, srg, sdz, sga, sgb, dn_norm_w, w_ret, w_dn, w_o, layer, o_f, o_b,
               tm, tiles_per_batch, batch, l, lc):
    nt = xs.shape[0]
    n_lat_tiles = tiles_per_batch * batch
    assert nt // tm == n_lat_tiles + 1 and batch * lc == tm
    row = lambda w: pl.BlockSpec((tm, w), lambda i: (i, 0))
    lat = pl.BlockSpec((tm, VW), lambda i: (jnp.where(i < n_lat_tiles, i % tiles_per_batch, 0),
                                            jnp.where(i < n_lat_tiles, i // tiles_per_batch, 0)))
    ctx_specs, ctx_args = [], []
    for b in range(batch):
        for arr in (o_f, o_b):
            ctx_specs.append(pl.BlockSpec((lc, VW), functools.partial(lambda i, b: (l // lc, b), b=b)))
            ctx_args.append(arr)
    return pl.pallas_call(
        functools.partial(_out_kernel, batch=batch, n_lat_tiles=n_lat_tiles),
        grid=(nt // tm,),
        in_specs=[
            row(D_MODEL), _mod_spec(tiles_per_batch, batch),
            row(VW), row(VW), row(VW), row(D_MODEL), row(D_MODEL),
            _const_spec((1, DV)),
            _layer_spec((VW, D_MODEL), layer), _layer_spec((VW, D_MODEL), layer),
            _layer_spec((D_MODEL, D_MODEL), layer),
            lat, lat,
        ] + ctx_specs,
        out_specs=row(D_MODEL),
        out_shape=jax.ShapeDtypeStruct((nt, D_MODEL), F32),
        compiler_params=_params(("parallel",)),
        name="mixer_out",
    )(xs, mods, ret, srg, sdz, sga, sgb, dn_norm_w.reshape(1, DV).astype(F32), w_ret, w_dn, w_o, o_f, o_b,
      *ctx_args)


def _rope_tables(l, tm):
    half = DK // 2
    n_freq = half // 2
    inv = ROPE_BASE ** (-jnp.arange(n_freq, dtype=F32) / n_freq)
    rows = l // GRID_W
    ang_r = jnp.arange(rows, dtype=F32)[:, None] * inv
    ang_c = jnp.arange(GRID_W, dtype=F32)[:, None] * inv

    def raster(fn):
        return jnp.concatenate([
            jnp.broadcast_to(fn(ang_r)[:, None, :], (rows, GRID_W, n_freq)),
            jnp.broadcast_to(fn(ang_c)[None, :, :], (rows, GRID_W, n_freq)),
        ], axis=-1).reshape(l, half)

    cos, sin = raster(jnp.cos), raster(jnp.sin)
    cos_t = jnp.concatenate([jnp.concatenate([cos, cos], axis=1), jnp.ones((tm, DK), F32)], axis=0)
    sin_t = jnp.concatenate([jnp.concatenate([-sin, sin], axis=1), jnp.zeros((tm, DK), F32)], axis=0)
    return cos_t, sin_t


def kernel(x, c, ctx, c_ctx, ada_w, ada_b, norm_w, ffn1_wgu, ffn1_wd, w_in, dn_conv_w, dn_a_log, dn_dt_bias,
           dn_norm_w, w_ret_out, w_dn_out, w_o, ffn2_wgu, ffn2_wd, final_norm_w):
    batch, l, _ = x.shape
    lc = ctx.shape[1]
    depth = ada_w.shape[0]
    assert batch + 1 <= SUBLANES and l % DN_CHUNK == 0 and lc % DN_CHUNK == 0 and l % GRID_W == 0
    tm = _row_tile(l, batch * lc)
    tiles_per_batch = l // tm

    assert batch * lc == tm
    xs = x.reshape(batch * l, D_MODEL)
    ctx_rows = ctx.reshape(batch * lc, D_MODEL)
    cmat =jnp.zeros((SUBLANES, D_MODEL), F32).at[:batch].set(c).at[batch].set(c_ctx)
    mods_all = _ada_all(cmat, ada_w, ada_b).reshape(depth, SUBLANES, N_MOD, D_MODEL)
    cos_t, sin_t = _rope_tables(l, tm)
    log_gamma = jnp.log1p(-jnp.power(2.0, -5.0 - jnp.arange(HEADS, dtype=F32)))
    log_gamma_t = jnp.broadcast_to(log_gamma[:, None, None], (HEADS, 1, LANES))

    w_main_all = w_in[:, :, :_OFF_GA].astype(BF16)
    w_gates_all = w_in[:, :, _OFF_GA + N_BA:].astype(BF16)
    w_ba_cols = w_in[:, :, _OFF_GA:_OFF_GA + N_BA]
    w_ba_all = jnp.pad(w_ba_cols, ((0, 0), (0, 0), (0, LANES - N_BA))).astype(BF16)
    w_bat_all = jnp.swapaxes(w_ba_cols, 1, 2).astype(BF16)
    f1_wgu, f1_wd, f2_wgu, f2_wd = (w.astype(BF16) for w in (ffn1_wgu, ffn1_wd, ffn2_wgu, ffn2_wd))
    wr16, wdn16, wo16 = (w.astype(BF16) for w in (w_ret_out, w_dn_out, w_o))

    for layer in range(depth):
        mods = mods_all[layer, :batch + 1]
        xs = _ffn(xs, mods, norm_w[layer, 0], f1_wgu, f1_wd, layer,
                  final_norm_w, 0, tm, tiles_per_batch, batch, ctx_rows=ctx_rows if layer == 0 else None)
        rq, rk, rv, srg, dpre, sdz, ba, bat, sga, sgb = _inproj(
            xs, mods, norm_w[layer, 1], w_main_all, w_gates_all, w_ba_all, w_bat_all, layer, cos_t, sin_t,
            tm, tiles_per_batch, batch)
        ret = _retention(rq, rk, rv, log_gamma_t, batch, l, lc)
        p1, p2, u, eg = _dnprep(dpre, ba, bat, dn_conv_w[layer], dn_a_log[layer], dn_dt_bias[layer], batch, l, lc)
        o_f, o_b = _dnseq(p1, p2, u, eg, batch, l, lc)
        xs = _mixer_out(xs, mods, ret, srg, sdz, sga, sgb, dn_norm_w[layer], wr16, wdn16, wo16, layer, o_f, o_b,
                        tm, tiles_per_batch, batch, l, lc)
        xs = _ffn(xs, mods, norm_w[layer, 2], f2_wgu, f2_wd, layer,
                  final_norm_w, 6, tm, tiles_per_batch, batch, final=layer == depth - 1)

    return xs.reshape(batch, l, D_MODEL)
```
